```python
import math
import jax, jax.numpy as jnp
from jax import lax
import numpy as np

D_MODEL = 2048
BATCH = 4
SEQ = 2048
DEPTH = 4

CHUNK = 64
Q_BLOCK = 128
HEAD_DIM = 128
H_MLA = 8
MLA_NOPE = 128
MLA_ROPE = 64
MLA_V = 128
Q_LORA = 512
KV_LORA = 256
ROPE_THETA = 10000.0
H_FOX = 8
H_SB = 8
BRANCH_W = H_MLA * MLA_V
N_BRANCH = 3
DEEPNORM_ALPHA = (2 * DEPTH) ** 0.25
DEEPNORM_BETA = (8 * DEPTH) ** -0.25
LN_EPS = 1e-5
RMS_EPS = 1e-6
NEG_INF = -1e30
SPLIT_SIZES = (Q_LORA, KV_LORA, MLA_ROPE, BRANCH_W,
               3 * BRANCH_W, H_FOX, BRANCH_W,
               3 * BRANCH_W, BRANCH_W,
               N_BRANCH * D_MODEL)
D_IN = sum(SPLIT_SIZES)

kernel_name = 'hybrid_mla_fox_stickbreaking_deepnorm_adaln'


def layer_norm(x, g=None, b=None):
    xf = x.astype(jnp.float32)
    mu = jnp.mean(xf, axis=-1, keepdims=True)
    var = jnp.mean(jnp.square(xf - mu), axis=-1, keepdims=True)
    y = (xf - mu) * lax.rsqrt(var + LN_EPS)
    if g is not None:
        y = y * g.astype(jnp.float32) + b.astype(jnp.float32)
    return y.astype(x.dtype)


def rms_norm(x, g):
    xf = x.astype(jnp.float32)
    y = xf * lax.rsqrt(jnp.mean(xf * xf, axis=-1, keepdims=True) + RMS_EPS)
    return (y * g.astype(jnp.float32)).astype(x.dtype)


def rope(x, positions):
    half = x.shape[-1] // 2
    inv_freq = ROPE_THETA ** (-jnp.arange(half, dtype=jnp.float32) / half)
    ang = positions.astype(jnp.float32)[..., None] * inv_freq
    cos = jnp.cos(ang)[:, :, None, :]
    sin = jnp.sin(ang)[:, :, None, :]
    xf = x.astype(jnp.float32)
    x1, x2 = xf[..., :half], xf[..., half:]
    return jnp.concatenate([x1 * cos - x2 * sin, x2 * cos + x1 * sin], axis=-1).astype(x.dtype)


def split_cols(p, sizes):
    idx, acc = [], 0
    for s in sizes[:-1]:
        acc += s
        idx.append(acc)
    return jnp.split(p, idx, axis=-1)


def sweep_query_blocks(block_fn, q, k, v):
    outs = []
    for i in range(q.shape[1] // Q_BLOCK):
        q0 = i * Q_BLOCK
        q1 = q0 + Q_BLOCK
        outs.append(block_fn(q[:, q0:q1], k[:, :q1], v[:, :q1], q0))
    return jnp.concatenate(outs, axis=1)


def mla_block(q, k, v, q0):
    s = jnp.einsum('bqhd,bkhd->bhqk', q, k).astype(jnp.float32) / math.sqrt(MLA_NOPE + MLA_ROPE)
    t = q0 + jnp.arange(q.shape[1])
    src = jnp.arange(k.shape[1])
    mask = (src[None, :] // CHUNK) <= (t[:, None] // CHUNK)
    p = jax.nn.softmax(jnp.where(mask, s, NEG_INF), axis=-1).astype(v.dtype)
    return jnp.einsum('bhqk,bkhd->bqhd', p, v)


def fox_block(q, k, v, q0, fcum):
    n_q, n_k = q.shape[1], k.shape[1]
    s = jnp.einsum('bqhd,bkhd->bhqk', q, k).astype(jnp.float32) / math.sqrt(HEAD_DIM)
    s = s + fcum[:, :, q0:q0 + n_q, None] - fcum[:, :, None, :n_k]
    t = q0 + jnp.arange(n_q)
    src = jnp.arange(n_k)
    mask = src[None, :] <= t[:, None]
    p = jax.nn.softmax(jnp.where(mask, s, NEG_INF), axis=-1).astype(v.dtype)
    return jnp.einsum('bhqk,bkhd->bqhd', p, v)


def stick_breaking_block(q, k, v, q0):
    z = jnp.einsum('bqhd,bkhd->bhqk', q, k).astype(jnp.float32) / math.sqrt(HEAD_DIM)
    t = q0 + jnp.arange(q.shape[1])
    src = jnp.arange(k.shape[1])
    strict = src[None, :] < t[:, None]
    log_beta = jax.nn.log_sigmoid(z)
    log_1mb = jnp.where(strict, jax.nn.log_sigmoid(-z), 0.0)
    after = lax.cumsum(log_1mb, axis=log_1mb.ndim - 1, reverse=True) - log_1mb
    a = jnp.where(strict, jnp.exp(log_beta + after), 0.0).astype(v.dtype)
    return jnp.einsum('bhqk,bkhd->bqhd', a, v)


def hybrid_layer(x, c, positions, w_ada, b_ada, w_in, q_norm_g, kv_norm_g, w_uq, w_ukv,
                 fox_bias, w_branch, w_out, ln_g, ln_b):
    bsz, seq, _ = x.shape
    mod = c @ w_ada + b_ada
    shift, scale, gate = jnp.split(mod, 3, axis=-1)
    u = layer_norm(x) * (1.0 + scale[:, None, :]) + shift[:, None, :]

    proj = u @ w_in
    (c_q, c_kv, k_rope, g_a, qkv_b, f_logit, g_b, qkv_c, g_c, merge_logit) = split_cols(proj, SPLIT_SIZES)

    q_a = (rms_norm(c_q, q_norm_g) @ w_uq).reshape(bsz, seq, H_MLA, MLA_NOPE + MLA_ROPE)
    q_a = jnp.concatenate([q_a[..., :MLA_NOPE], rope(q_a[..., MLA_NOPE:], positions)], axis=-1)
    kv_a = (rms_norm(c_kv, kv_norm_g) @ w_ukv).reshape(bsz, seq, H_MLA, MLA_NOPE + MLA_V)
    k_pe = jnp.broadcast_to(rope(k_rope[:, :, None, :], positions), (bsz, seq, H_MLA, MLA_ROPE))
    k_a = jnp.concatenate([kv_a[..., :MLA_NOPE], k_pe], axis=-1)
    v_a = kv_a[..., MLA_NOPE:]
    o_a = sweep_query_blocks(mla_block, q_a, k_a, v_a).reshape(bsz, seq, BRANCH_W)

    q_b, k_b, v_b = [t.reshape(bsz, seq, H_FOX, HEAD_DIM) for t in jnp.split(qkv_b, 3, axis=-1)]
    log_f = jax.nn.log_sigmoid(f_logit.astype(jnp.float32) + fox_bias.astype(jnp.float32))
    fcum = jnp.transpose(jnp.cumsum(log_f, axis=1), (0, 2, 1))
    o_b = sweep_query_blocks(lambda qb, kb, vb, q0: fox_block(qb, kb, vb, q0, fcum),
                             q_b, k_b, v_b).reshape(bsz, seq, BRANCH_W)

    q_c, k_c, v_c = [t.reshape(bsz, seq, H_SB, HEAD_DIM) for t in jnp.split(qkv_c, 3, axis=-1)]
    o_c = sweep_query_blocks(stick_breaking_block, q_c, k_c, v_c).reshape(bsz, seq, BRANCH_W)

    ys = jnp.stack([o_a * jax.nn.silu(g_a), o_b * jax.nn.silu(g_b), o_c * jax.nn.silu(g_c)], axis=2)
    branches = jnp.einsum('bsnw,nwd->bsnd', ys, w_branch)
    merge = jax.nn.sigmoid(merge_logit).reshape(bsz, seq, N_BRANCH, D_MODEL)
    merged = jnp.sum(merge * branches, axis=2)
    out = merged @ w_out

    return layer_norm(DEEPNORM_ALPHA * x + gate[:, None, :] * out, ln_g, ln_b)


def setup_inputs(seed: int = 0) -> dict:
    key = jax.random.key(seed)
    ks = jax.random.split(key, 16)
    f32 = jnp.float32
    x = jax.random.normal(ks[0], (BATCH, SEQ, D_MODEL), f32)
    c = jax.random.normal(ks[1], (BATCH, D_MODEL), f32)
    start = jax.random.randint(ks[2], (BATCH, 1), 0, 64, dtype=jnp.int32) * CHUNK
    positions = (start + jnp.arange(SEQ, dtype=jnp.int32)[None, :]).astype(jnp.int32)
    w_ada = jax.random.normal(ks[3], (DEPTH, D_MODEL, 3 * D_MODEL), f32) * (0.5 * D_MODEL ** -0.5)
    b_ada = jax.random.normal(ks[4], (DEPTH, 3 * D_MODEL), f32) * 0.02
    w_in = jax.random.normal(ks[5], (DEPTH, D_MODEL, D_IN), f32) * D_MODEL ** -0.5
    q_norm_g = 1.0 + 0.05 * jax.random.normal(ks[6], (DEPTH, Q_LORA), f32)
    kv_norm_g = 1.0 + 0.05 * jax.random.normal(ks[7], (DEPTH, KV_LORA), f32)
    w_uq = jax.random.normal(ks[8], (DEPTH, Q_LORA, H_MLA * (MLA_NOPE + MLA_ROPE)), f32) * Q_LORA ** -0.5
    w_ukv = jax.random.normal(ks[9], (DEPTH, KV_LORA, H_MLA * (MLA_NOPE + MLA_V)), f32) * KV_LORA ** -0.5
    fox_bias = jax.random.uniform(ks[10], (DEPTH, H_FOX), f32, 1.0, 4.0)
    w_branch = jax.random.normal(ks[11], (DEPTH, N_BRANCH, BRANCH_W, D_MODEL), f32) * (DEEPNORM_BETA * BRANCH_W ** -0.5)
    w_out = jax.random.normal(ks[12], (DEPTH, D_MODEL, D_MODEL), f32) * (DEEPNORM_BETA * D_MODEL ** -0.5)
    ln_g = 1.0 + 0.05 * jax.random.normal(ks[13], (DEPTH, D_MODEL), f32)
    ln_b = 0.02 * jax.random.normal(ks[14], (DEPTH, D_MODEL), f32)
    return {'x': x, 'c': c, 'positions': positions, 'w_ada': w_ada, 'b_ada': b_ada, 'w_in': w_in,
            'q_norm_g': q_norm_g, 'kv_norm_g': kv_norm_g, 'w_uq': w_uq, 'w_ukv': w_ukv,
            'fox_bias': fox_bias, 'w_branch': w_branch, 'w_out': w_out, 'ln_g': ln_g, 'ln_b': ln_b}


def reference(x, c, positions, w_ada, b_ada, w_in, q_norm_g, kv_norm_g, w_uq, w_ukv,
              fox_bias, w_branch, w_out, ln_g, ln_b):
    for l in range(DEPTH):
        x = hybrid_layer(x, c, positions, w_ada[l], b_ada[l], w_in[l], q_norm_g[l], kv_norm_g[l],
                         w_uq[l], w_ukv[l], fox_bias[l], w_branch[l], w_out[l], ln_g[l], ln_b[l])
    return x
```

```python
import functools
import math

import jax
import jax.numpy as jnp
from jax import lax
from jax.experimental import pallas as pl
from jax.experimental.pallas import tpu as pltpu

F32 = jnp.float32
BF16 = jnp.bfloat16

N_HEADS = 8
HEAD_DIM = 128
MLA_NOPE = 128
MLA_ROPE = 64
MLA_QK = 2 * HEAD_DIM
Q_LORA = 512
KV_LORA = 256
CHUNK = 64
N_BRANCH = 3
BRANCH_W = N_HEADS * HEAD_DIM
ROPE_THETA = 10000.0
LN_EPS = 1e-5
RMS_EPS = 1e-6
NEG_INF = -1e30
LANES = 128
LAT_W = 1024
VMEM_LIMIT = 52 * 1024 * 1024

ATT_BLOCK = 256
LAT_ROWS = 256


def _params(*sem):
    return pltpu.CompilerParams(dimension_semantics=sem, vmem_limit_bytes=VMEM_LIMIT)


def _log_sigmoid(x):
    return jnp.minimum(x, 0.0) - jnp.log1p(jnp.exp(-jnp.abs(x)))


def _ada_kernel(c_ref, w_ref, b_ref, o_ref):
    w = w_ref[0].astype(BF16)
    o_ref[0] = jnp.dot(c_ref[...], w, preferred_element_type=F32) + b_ref[0]


def _ada_call(c_pad, w_ada, b_ada):
    depth, d, n3 = w_ada.shape
    tn = 1024
    return pl.pallas_call(
        _ada_kernel,
        grid=(depth, n3 // tn),
        in_specs=[pl.BlockSpec((8, d), lambda l, j: (0, 0)),
                  pl.BlockSpec((1, d, tn), lambda l, j: (l, 0, j)),
                  pl.BlockSpec((1, 1, tn), lambda l, j: (l, 0, j))],
        out_specs=pl.BlockSpec((1, 8, tn), lambda l, j: (l, 0, j)),
        out_shape=jax.ShapeDtypeStruct((depth, 8, n3), F32),
        compiler_params=_params("arbitrary", "arbitrary"),
        name="ada_mod",
    )(c_pad, w_ada, b_ada.reshape(depth, 1, n3))


def _rope_kernel(pos_ref, invf_ref, cc_ref, ss_ref):
    ang = pos_ref[0].astype(F32) * invf_ref[...]
    lane = lax.broadcasted_iota(jnp.int32, ang.shape, 1)
    cos = jnp.cos(ang)
    sin = jnp.sin(ang)
    half = MLA_ROPE // 2
    cc_ref[0] = jnp.where(lane < MLA_ROPE, cos, 0.0)
    ss_ref[0] = jnp.where(lane < half, -sin, jnp.where(lane < MLA_ROPE, sin, 0.0))


def _rope_call(positions, invf):
    b, s = positions.shape
    ts = 512
    spec = pl.BlockSpec((1, ts, LANES), lambda i, j: (i, j, 0))
    return pl.pallas_call(
        _rope_kernel,
        grid=(b, s // ts),
        in_specs=[pl.BlockSpec((1, ts, 1), lambda i, j: (i, j, 0)),
                  pl.BlockSpec((1, LANES), lambda i, j: (0, 0))],
        out_specs=[spec, spec],
        out_shape=[jax.ShapeDtypeStruct((b, s, LANES), F32)] * 2,
        compiler_params=_params("arbitrary", "arbitrary"),
        name="rope_tables",
    )(positions.reshape(b, s, 1), invf)


def _rope_apply(r, cc, ss):
    return r * cc + pltpu.roll(r, MLA_ROPE, 1) * ss


def _lat_kernel(x_ref, shift_ref, scale_ref, wlat_ref, qg_ref, kvg_ref, wuq_ref, wukv_ref,
                cc_ref, ss_ref, fb_ref,
                u_ref, q_ref, k_ref, v_ref, fcol_ref, frow_ref, carry_ref):
    ts = x_ref.shape[1]
    x = x_ref[0]
    mu = jnp.mean(x, axis=-1, keepdims=True)
    xc = x - mu
    var = jnp.mean(xc * xc, axis=-1, keepdims=True)
    y = xc * lax.rsqrt(var + LN_EPS)
    u = (y * (1.0 + scale_ref[0]) + shift_ref[0]).astype(BF16)
    u_ref[0] = u

    lat = jnp.dot(u, wlat_ref[...], preferred_element_type=F32)
    cc = cc_ref[0]
    ss = ss_ref[0]
    q_scale = 1.0 / math.sqrt(MLA_NOPE + MLA_ROPE)

    cq = lat[:, :Q_LORA]
    cqn = cq * lax.rsqrt(jnp.mean(cq * cq, axis=-1, keepdims=True) + RMS_EPS) * qg_ref[...]
    q = jnp.dot(cqn.astype(BF16), wuq_ref[...], preferred_element_type=F32)
    for h in range(N_HEADS):
        base = h * MLA_QK
        q_ref[h, 0, :, 0:HEAD_DIM] = (q[:, base:base + HEAD_DIM] * q_scale).astype(BF16)
        r = q[:, base + HEAD_DIM:base + MLA_QK]
        q_ref[h, 0, :, HEAD_DIM:MLA_QK] = (_rope_apply(r, cc, ss) * q_scale).astype(BF16)

    ckv = lat[:, Q_LORA:Q_LORA + KV_LORA]
    ckvn = ckv * lax.rsqrt(jnp.mean(ckv * ckv, axis=-1, keepdims=True) + RMS_EPS) * kvg_ref[...]
    kv = jnp.dot(ckvn.astype(BF16), wukv_ref[...], preferred_element_type=F32)
    r_off = Q_LORA + KV_LORA
    kpe = _rope_apply(lat[:, r_off:r_off + LANES], cc, ss).astype(BF16)
    for h in range(N_HEADS):
        base = h * 2 * HEAD_DIM
        k_ref[h, 0, :, 0:HEAD_DIM] = kv[:, base:base + HEAD_DIM].astype(BF16)
        k_ref[h, 0, :, HEAD_DIM:MLA_QK] = kpe
        v_ref[h, 0] = kv[:, base + HEAD_DIM:base + 2 * HEAD_DIM].astype(BF16)

    f_off = r_off + LANES
    logf = _log_sigmoid(lat[:, f_off:f_off + LANES] + fb_ref[...])
    lane = lax.broadcasted_iota(jnp.int32, logf.shape, 1)
    logf = jnp.where(lane < N_HEADS, logf, 0.0)
    row = lax.broadcasted_iota(jnp.int32, (ts, ts), 0)
    col = lax.broadcasted_iota(jnp.int32, (ts, ts), 1)
    tri = jnp.where(row >= col, 1.0, 0.0).astype(BF16)
    p1 = logf.astype(BF16)
    r1 = logf - p1.astype(F32)
    p2 = r1.astype(BF16)
    p3 = (r1 - p2.astype(F32)).astype(BF16)
    cs = (jnp.dot(tri, p1, preferred_element_type=F32)
          + jnp.dot(tri, p2, preferred_element_type=F32)
          + jnp.dot(tri, p3, preferred_element_type=F32))

    @pl.when(pl.program_id(1) == 0)
    def _():
        carry_ref[...] = jnp.zeros_like(carry_ref)

    fc = cs + carry_ref[...]
    carry_ref[...] = fc[ts - 1:ts, :]
    fcol_ref[0] = fc
    frow_ref[0, 0] = fc.T[:N_HEADS, :]


def _lat_call(x, shift, scale, wlat, qg, kvg, wuq, wukv, cc, ss, fb):
    b, s, d = x.shape
    ts = LAT_ROWS
    const = lambda shape: pl.BlockSpec(shape, lambda i, j: (0,) * len(shape))
    tok = lambda w: pl.BlockSpec((1, ts, w), lambda i, j: (i, j, 0))
    per_batch = pl.BlockSpec((1, 1, d), lambda i, j: (i, 0, 0))
    head = lambda w: pl.BlockSpec((N_HEADS, 1, ts, w), lambda i, j: (0, i, j, 0))
    return pl.pallas_call(
        _lat_kernel,
        grid=(b, s // ts),
        in_specs=[tok(d), per_batch, per_batch, const((d, LAT_W)), const((1, Q_LORA)), const((1, KV_LORA)),
                  const((Q_LORA, N_HEADS * MLA_QK)), const((KV_LORA, N_HEADS * 2 * HEAD_DIM)),
                  tok(LANES), tok(LANES), const((1, LANES))],
        out_specs=[tok(d), head(MLA_QK), head(MLA_QK), head(HEAD_DIM), tok(LANES),
                   pl.BlockSpec((1, 1, N_HEADS, ts), lambda i, j: (i, j, 0, 0))],
        out_shape=[jax.ShapeDtypeStruct((b, s, d), BF16),
                   jax.ShapeDtypeStruct((N_HEADS, b, s, MLA_QK), BF16),
                   jax.ShapeDtypeStruct((N_HEADS, b, s, MLA_QK), BF16),
                   jax.ShapeDtypeStruct((N_HEADS, b, s, HEAD_DIM), BF16),
                   jax.ShapeDtypeStruct((b, s, LANES), F32),
                   jax.ShapeDtypeStruct((b, s // ts, N_HEADS, ts), F32)],
        scratch_shapes=[pltpu.VMEM((1, LANES), F32)],
        compiler_params=_params("arbitrary", "arbitrary"),
        name="ln_latent",
    )(x, shift, scale, wlat, qg, kvg, wuq, wukv, cc, ss, fb)


def _proj_kernel(u_ref, w_ref, o_ref):
    acc = jnp.dot(u_ref[...], w_ref[...], preferred_element_type=F32)
    for g in range(o_ref.shape[0]):
        o_ref[g] = acc[:, g * LANES:(g + 1) * LANES].astype(BF16)


def _proj_call(u, w):
    n, d = u.shape
    ncol = w.shape[1]
    tm, tn = 1024, 1024
    return pl.pallas_call(
        _proj_kernel,
        grid=(n // tm, ncol // tn),
        in_specs=[pl.BlockSpec((tm, d), lambda i, j: (i, 0)),
                  pl.BlockSpec((d, tn), lambda i, j: (0, j))],
        out_specs=pl.BlockSpec((tn // LANES, tm, LANES), lambda i, j: (j, i, 0)),
        out_shape=jax.ShapeDtypeStruct((ncol // LANES, n, LANES), BF16),
        compiler_params=_params("arbitrary", "arbitrary"),
        name="in_proj",
    )(u, w)


def _block_iotas(t):
    row = lax.broadcasted_iota(jnp.int32, (t, t), 0)
    col = lax.broadcasted_iota(jnp.int32, (t, t), 1)
    return row, col


def _qk(q, k):
    return lax.dot_general(q, k, (((1,), (1,)), ((), ())), preferred_element_type=F32)


def _softmax_kernel(*refs, forget):
    if forget:
        q_ref, k_ref, v_ref, fcol_ref, frow_ref, o_ref = refs
    else:
        q_ref, k_ref, v_ref, o_ref = refs
    t = q_ref.shape[2]
    qi = pl.program_id(1)
    row, col = _block_iotas(t)
    if forget:
        mask = col <= row
        lane = lax.broadcasted_iota(jnp.int32, (t, LANES), 1)
        fcol = fcol_ref[0]
    else:
        mask = (col // CHUNK) <= (row // CHUNK)

    def head(h, _):
        q = q_ref[h, 0]
        if forget:
            fq = jnp.sum(jnp.where(lane == h, fcol, 0.0), axis=-1, keepdims=True)

        def step(j, carry, diagonal):
            m, l, acc = carry
            off = pl.multiple_of(j * t, t)
            s = _qk(q, k_ref[h, 0, pl.ds(off, t), :])
            if forget:
                s = s + (fq - frow_ref[0, j, pl.ds(h, 1), :])
            if diagonal:
                s = jnp.where(mask, s, NEG_INF)
            m_new = jnp.maximum(m, jnp.max(s, axis=-1, keepdims=True))
            p = jnp.exp(s - m_new)
            alpha = jnp.exp(m - m_new)
            l = alpha * l + jnp.sum(p, axis=-1, keepdims=True)
            acc = alpha * acc + jnp.dot(p.astype(BF16), v_ref[h, 0, pl.ds(off, t), :],
                                        preferred_element_type=F32)
            return m_new, l, acc

        init = (jnp.full((t, 1), NEG_INF, F32), jnp.zeros((t, 1), F32), jnp.zeros((t, HEAD_DIM), F32))
        carry = lax.fori_loop(0, qi, functools.partial(step, diagonal=False), init)
        m, l, acc = step(qi, carry, True)
        o_ref[h, 0] = (acc / l).astype(BF16)
        return 0

    lax.fori_loop(0, N_HEADS, head, 0)


def _stick_kernel(q_ref, k_ref, v_ref, o_ref):
    t = q_ref.shape[2]
    qi = pl.program_id(1)
    row, col = _block_iotas(t)
    strict = col < row
    later = jnp.where(row > col, 1.0, 0.0).astype(BF16)

    def head(h, _):
        q = q_ref[h, 0]

        def step(j, carry, diagonal):
            c, acc = carry
            off = pl.multiple_of(j * t, t)
            z = _qk(q, k_ref[h, 0, pl.ds(off, t), :])
            log_beta = _log_sigmoid(z)
            log_1mb = log_beta - z
            if diagonal:
                log_1mb = jnp.where(strict, log_1mb, 0.0)
            hi = log_1mb.astype(BF16)
            lo = (log_1mb - hi.astype(F32)).astype(BF16)
            after = (jnp.dot(hi, later, preferred_element_type=F32)
                     + jnp.dot(lo, later, preferred_element_type=F32)) + c
            a = jnp.exp(log_beta + after)
            if diagonal:
                a = jnp.where(strict, a, 0.0)
            acc = acc + jnp.dot(a.astype(BF16), v_ref[h, 0, pl.ds(off, t), :],
                                preferred_element_type=F32)
            c = c + jnp.sum(log_1mb, axis=-1, keepdims=True)
            return c, acc

        carry = step(qi, (jnp.zeros((t, 1), F32), jnp.zeros((t, HEAD_DIM), F32)), True)
        c, acc = lax.fori_loop(0, qi, lambda i, cr: step(qi - 1 - i, cr, False), carry)
        o_ref[h, 0] = acc.astype(BF16)
        return 0

    lax.fori_loop(0, N_HEADS, head, 0)


def _attn_call(kernel, name, q, k, v, q_blk, k_blk, v_blk, extra=(), extra_specs=()):
    _, b, s, _ = q.shape
    t = ATT_BLOCK
    qw, kw, vw = q.shape[3], k.shape[3], v.shape[3]
    return pl.pallas_call(
        kernel,
        grid=(b, s // t),
        in_specs=[pl.BlockSpec((N_HEADS, 1, t, qw), lambda i, j: (q_blk, i, j, 0)),
                  pl.BlockSpec((N_HEADS, 1, s, kw), lambda i, j: (k_blk, i, 0, 0)),
                  pl.BlockSpec((N_HEADS, 1, s, vw), lambda i, j: (v_blk, i, 0, 0)),
                  *extra_specs],
        out_specs=pl.BlockSpec((N_HEADS, 1, t, HEAD_DIM), lambda i, j: (0, i, j, 0)),
        out_shape=jax.ShapeDtypeStruct((N_HEADS, b, s, HEAD_DIM), BF16),
        compiler_params=_params("arbitrary", "arbitrary"),
        name=name,
    )(q, k, v, *extra)


def _merge_kernel(oa_ref, ob_ref, oc_ref, ga_ref, gb_ref, gc_ref, ma_ref, mb_ref, mc_ref, wb_ref,
                  out_ref, ys_ref):
    @pl.when(pl.program_id(1) == 0)
    def _():
        for n, (o_ref, g_ref) in enumerate(((oa_ref, ga_ref), (ob_ref, gb_ref), (oc_ref, gc_ref))):
            for h in range(N_HEADS):
                g = g_ref[h].astype(F32)
                ys_ref[n, :, h * HEAD_DIM:(h + 1) * HEAD_DIM] = (
                    o_ref[h].astype(F32) * (g * jax.nn.sigmoid(g))).astype(BF16)

    branches = [jnp.dot(ys_ref[n], wb_ref[n], preferred_element_type=F32) for n in range(N_BRANCH)]
    for g in range(ma_ref.shape[0]):
        sl = slice(g * LANES, (g + 1) * LANES)
        acc = jax.nn.sigmoid(ma_ref[g].astype(F32)) * branches[0][:, sl]
        acc = acc + jax.nn.sigmoid(mb_ref[g].astype(F32)) * branches[1][:, sl]
        acc = acc + jax.nn.sigmoid(mc_ref[g].astype(F32)) * branches[2][:, sl]
        out_ref[:, sl] = acc.astype(BF16)


def _merge_call(o_a, o_b, o_c, proj, wb, blk):
    n = proj.shape[1]
    d = wb.shape[2]
    tm, tn = 512, 512
    gpt = tn // LANES
    per_branch = d // tn
    heads = lambda c: pl.BlockSpec((N_HEADS, tm, LANES), lambda i, j: (c, i, 0))
    gate = lambda b: pl.BlockSpec((gpt, tm, LANES), lambda i, j: (b * per_branch + j, i, 0))
    return pl.pallas_call(
        _merge_kernel,
        grid=(n // tm, d // tn),
        in_specs=[heads(0), heads(0), heads(0), heads(blk["g_a"]), heads(blk["g_b"]), heads(blk["g_c"]),
                  gate(0), gate(1), gate(2),
                  pl.BlockSpec((N_BRANCH, BRANCH_W, tn), lambda i, j: (0, 0, j))],
        out_specs=pl.BlockSpec((tm, tn), lambda i, j: (i, j)),
        out_shape=jax.ShapeDtypeStruct((n, d), BF16),
        scratch_shapes=[pltpu.VMEM((N_BRANCH, tm, BRANCH_W), BF16)],
        compiler_params=_params("arbitrary", "arbitrary"),
        name="gate_branch_merge",
    )(o_a, o_b, o_c, proj, proj, proj, proj, proj, proj, wb)


def _out_kernel(m_ref, w_ref, x_ref, gate_ref, g_ref, b_ref, o_ref, *, alpha):
    out = jnp.dot(m_ref[...], w_ref[...], preferred_element_type=F32)
    r = alpha * x_ref[0] + gate_ref[0] * out
    mu = jnp.mean(r, axis=-1, keepdims=True)
    rc = r - mu
    var = jnp.mean(rc * rc, axis=-1, keepdims=True)
    o_ref[0] = rc * lax.rsqrt(var + LN_EPS) * g_ref[...] + b_ref[...]


def _out_call(merged, w_out, x, gate, ln_g, ln_b, alpha):
    b, s, d = x.shape
    tm = 512
    per_seq = s // tm
    return pl.pallas_call(
        functools.partial(_out_kernel, alpha=alpha),
        grid=(b, per_seq),
        in_specs=[pl.BlockSpec((tm, d), lambda i, j: (i * per_seq + j, 0)),
                  pl.BlockSpec((d, d), lambda i, j: (0, 0)),
                  pl.BlockSpec((1, tm, d), lambda i, j: (i, j, 0)),
                  pl.BlockSpec((1, 1, d), lambda i, j: (i, 0, 0)),
                  pl.BlockSpec((1, d), lambda i, j: (0, 0)),
                  pl.BlockSpec((1, d), lambda i, j: (0, 0))],
        out_specs=pl.BlockSpec((1, tm, d), lambda i, j: (i, j, 0)),
        out_shape=jax.ShapeDtypeStruct((b, s, d), F32),
        compiler_params=_params("arbitrary", "arbitrary"),
        name="out_proj_ln",
    )(merged, w_out, x, gate, ln_g, ln_b)


def _swap_halves(w):
    half = w.shape[-1] // 2
    return jnp.concatenate([w[..., half:], w[..., :half]], axis=-1)


def _prepare_weights(w_in, w_uq, w_ukv, w_branch, w_out):
    depth, d, _ = w_in.shape
    sizes = (Q_LORA, KV_LORA, MLA_ROPE, BRANCH_W, 3 * BRANCH_W, N_HEADS, BRANCH_W, 3 * BRANCH_W, BRANCH_W,
             N_BRANCH * d)
    parts, off = [], 0
    for size in sizes:
        parts.append(w_in[:, :, off:off + size])
        off += size
    c_q, c_kv, k_rope, g_a, qkv_b, f_logit, g_b, qkv_c, g_c, merge = parts
    pad = jnp.zeros((depth, d, LANES - N_HEADS), w_in.dtype)
    w_lat = jnp.concatenate([c_q, c_kv, k_rope, _swap_halves(k_rope), f_logit, pad], axis=-1).astype(BF16)
    s = 1.0 / math.sqrt(HEAD_DIM)
    w_main = jnp.concatenate([merge, g_a, qkv_b[..., :BRANCH_W] * s, qkv_b[..., BRANCH_W:], g_b,
                              qkv_c[..., :BRANCH_W] * s, qkv_c[..., BRANCH_W:], g_c], axis=-1).astype(BF16)
    uq = w_uq.reshape(depth, Q_LORA, N_HEADS, MLA_NOPE + MLA_ROPE)
    rope_cols = uq[..., MLA_NOPE:]
    uq = jnp.concatenate([uq[..., :MLA_NOPE], rope_cols, _swap_halves(rope_cols)], axis=-1)
    uq = uq.reshape(depth, Q_LORA, N_HEADS * MLA_QK).astype(BF16)
    return w_lat, w_main, uq, w_ukv.astype(BF16), w_branch.astype(BF16), w_out.astype(BF16)


def _proj_blocks(d):
    first = N_BRANCH * d // (N_HEADS * LANES)
    names = ("g_a", "q_b", "k_b", "v_b", "g_b", "q_c", "k_c", "v_c", "g_c")
    return {name: first + i for i, name in enumerate(names)}


def kernel(x, c, positions, w_ada, b_ada, w_in, q_norm_g, kv_norm_g, w_uq, w_ukv, fox_bias, w_branch, w_out,
           ln_g, ln_b):
    b, s, d = x.shape
    depth = w_in.shape[0]
    alpha = (2 * depth) ** 0.25
    blk = _proj_blocks(d)

    w_lat, w_main, uq, ukv, wb, wo = _prepare_weights(w_in, w_uq, w_ukv, w_branch, w_out)

    c_pad = jnp.zeros((8, d), BF16).at[:b].set(c.astype(BF16))
    mod = _ada_call(c_pad, w_ada, b_ada)[:, :b].reshape(depth, b, 3, 1, d)

    half = MLA_ROPE // 2
    inv_freq = ROPE_THETA ** (-jnp.arange(half, dtype=F32) / half)
    invf = jnp.concatenate([inv_freq, inv_freq, jnp.zeros((LANES - MLA_ROPE,), F32)]).reshape(1, LANES)
    cc, ss = _rope_call(positions, invf)

    fb = jnp.zeros((depth, 1, LANES), F32).at[:, 0, :N_HEADS].set(fox_bias)

    for l in range(depth):
        shift, scale, gate = mod[l, :, 0], mod[l, :, 1], mod[l, :, 2]
        u, q_a, k_a, v_a, fcol, frow = _lat_call(
            x, shift, scale, w_lat[l], q_norm_g[l].reshape(1, -1), kv_norm_g[l].reshape(1, -1),
            uq[l], ukv[l], cc, ss, fb[l])
        proj = _proj_call(u.reshape(b * s, d), w_main[l])
        proj4 = proj.reshape(proj.shape[0], b, s, LANES)

        o_a = _attn_call(functools.partial(_softmax_kernel, forget=False), "mla_attention",
                         q_a, k_a, v_a, 0, 0, 0)
        t = ATT_BLOCK
        o_b = _attn_call(functools.partial(_softmax_kernel, forget=True), "fox_attention",
                         proj4, proj4, proj4, blk["q_b"], blk["k_b"], blk["v_b"],
                         extra=(fcol, frow),
                         extra_specs=(pl.BlockSpec((1, t, LANES), lambda i, j: (i, j, 0)),
                                      pl.BlockSpec((1, s // t, N_HEADS, t), lambda i, j: (i, 0, 0, 0))))
        o_c = _attn_call(_stick_kernel, "stick_attention",
                         proj4, proj4, proj4, blk["q_c"], blk["k_c"], blk["v_c"])

        merged = _merge_call(o_a.reshape(N_HEADS, b * s, LANES), o_b.reshape(N_HEADS, b * s, LANES),
                             o_c.reshape(N_HEADS, b * s, LANES), proj, wb[l], blk)
        x = _out_call(merged, wo[l], x, gate, ln_g[l].reshape(1, d), ln_b[l].reshape(1, d), alpha)
    return x
```

```python
import functools
import math

import jax
import jax.numpy as jnp
from jax import lax
from jax.experimental import pallas as pl
from jax.experimental.pallas import tpu as pltpu

F32 = jnp.float32
BF16 = jnp.bfloat16

N_HEADS = 8
HEAD_DIM = 128
MLA_NOPE = 128
MLA_ROPE = 64
MLA_QK = 2 * HEAD_DIM
Q_LORA = 512
KV_LORA = 256
CHUNK = 64
N_BRANCH = 3
BRANCH_W = N_HEADS * HEAD_DIM
ROPE_THETA = 10000.0
LN_EPS = 1e-5
RMS_EPS = 1e-6
NEG_INF = -1e30
LOG2E = math.log2(math.e)
LANES = 128
LAT_W = 1024
VMEM_LIMIT = 52 * 1024 * 1024

ATT_BLOCK = 256
LAT_ROWS = 256


def _params(*sem):
    return pltpu.CompilerParams(dimension_semantics=sem, vmem_limit_bytes=VMEM_LIMIT)


def _log_sigmoid(x):
    return jnp.minimum(x, 0.0) - jnp.log1p(jnp.exp(-jnp.abs(x)))


def _ada_kernel(c_ref, w_ref, b_ref, o_ref):
    w = w_ref[0].astype(BF16)
    o_ref[0] = jnp.dot(c_ref[...], w, preferred_element_type=F32) + b_ref[0]


def _ada_call(c_pad, w_ada, b_ada):
    depth, d, n3 = w_ada.shape
    tn = 1024
    return pl.pallas_call(
        _ada_kernel,
        grid=(depth, n3 // tn),
        in_specs=[pl.BlockSpec((8, d), lambda l, j: (0, 0)),
                  pl.BlockSpec((1, d, tn), lambda l, j: (l, 0, j)),
                  pl.BlockSpec((1, 1, tn), lambda l, j: (l, 0, j))],
        out_specs=pl.BlockSpec((1, 8, tn), lambda l, j: (l, 0, j)),
        out_shape=jax.ShapeDtypeStruct((depth, 8, n3), F32),
        compiler_params=_params("arbitrary", "arbitrary"),
        name="ada_mod",
    )(c_pad, w_ada, b_ada.reshape(depth, 1, n3))


def _rope_kernel(pos_ref, invf_ref, cc_ref, ss_ref):
    ang = pos_ref[0].astype(F32) * invf_ref[...]
    lane = lax.broadcasted_iota(jnp.int32, ang.shape, 1)
    cos = jnp.cos(ang)
    sin = jnp.sin(ang)
    half = MLA_ROPE // 2
    cc_ref[0] = jnp.where(lane < MLA_ROPE, cos, 0.0)
    ss_ref[0] = jnp.where(lane < half, -sin, jnp.where(lane < MLA_ROPE, sin, 0.0))


def _rope_call(positions, invf):
    b, s = positions.shape
    ts = 512
    spec = pl.BlockSpec((1, ts, LANES), lambda i, j: (i, j, 0))
    return pl.pallas_call(
        _rope_kernel,
        grid=(b, s // ts),
        in_specs=[pl.BlockSpec((1, ts, 1), lambda i, j: (i, j, 0)),
                  pl.BlockSpec((1, LANES), lambda i, j: (0, 0))],
        out_specs=[spec, spec],
        out_shape=[jax.ShapeDtypeStruct((b, s, LANES), F32)] * 2,
        compiler_params=_params("arbitrary", "arbitrary"),
        name="rope_tables",
    )(positions.reshape(b, s, 1), invf)


def _rope_apply(r, cc, ss):
    return r * cc + pltpu.roll(r, MLA_ROPE, 1) * ss


def _lat_kernel(x_ref, shift_ref, scale_ref, wlat_ref, qg_ref, kvg_ref, wuq_ref, wukv_ref,
                cc_ref, ss_ref, fb_ref,
                u_ref, q_ref, k_ref, v_ref, frep_ref, frow_ref, carry_ref):
    ts = x_ref.shape[1]
    x = x_ref[0]
    mu = jnp.mean(x, axis=-1, keepdims=True)
    xc = x - mu
    var = jnp.mean(xc * xc, axis=-1, keepdims=True)
    y = xc * lax.rsqrt(var + LN_EPS)
    u = (y * (1.0 + scale_ref[0]) + shift_ref[0]).astype(BF16)
    u_ref[0] = u

    lat = jnp.dot(u, wlat_ref[...], preferred_element_type=F32)
    cc = cc_ref[0]
    ss = ss_ref[0]
    q_scale = LOG2E / math.sqrt(MLA_NOPE + MLA_ROPE)

    cq = lat[:, :Q_LORA]
    cqn = cq * lax.rsqrt(jnp.mean(cq * cq, axis=-1, keepdims=True) + RMS_EPS) * qg_ref[...]
    q = jnp.dot(cqn.astype(BF16), wuq_ref[...], preferred_element_type=F32)
    for h in range(N_HEADS):
        base = h * MLA_QK
        q_ref[h, 0, :, 0:HEAD_DIM] = (q[:, base:base + HEAD_DIM] * q_scale).astype(BF16)
        r = q[:, base + HEAD_DIM:base + MLA_QK]
        q_ref[h, 0, :, HEAD_DIM:MLA_QK] = (_rope_apply(r, cc, ss) * q_scale).astype(BF16)

    ckv = lat[:, Q_LORA:Q_LORA + KV_LORA]
    ckvn = ckv * lax.rsqrt(jnp.mean(ckv * ckv, axis=-1, keepdims=True) + RMS_EPS) * kvg_ref[...]
    kv = jnp.dot(ckvn.astype(BF16), wukv_ref[...], preferred_element_type=F32)
    r_off = Q_LORA + KV_LORA
    kpe = _rope_apply(lat[:, r_off:r_off + LANES], cc, ss).astype(BF16)
    for h in range(N_HEADS):
        base = h * 2 * HEAD_DIM
        k_ref[h, 0, :, 0:HEAD_DIM] = kv[:, base:base + HEAD_DIM].astype(BF16)
        k_ref[h, 0, :, HEAD_DIM:MLA_QK] = kpe
        v_ref[h, 0] = kv[:, base + HEAD_DIM:base + 2 * HEAD_DIM].astype(BF16)

    f_off = r_off + LANES
    logf = _log_sigmoid(lat[:, f_off:f_off + LANES] + fb_ref[...])
    lane = lax.broadcasted_iota(jnp.int32, logf.shape, 1)
    logf = jnp.where(lane < N_HEADS, logf, 0.0)
    row = lax.broadcasted_iota(jnp.int32, (ts, ts), 0)
    col = lax.broadcasted_iota(jnp.int32, (ts, ts), 1)
    tri = jnp.where(row >= col, 1.0, 0.0).astype(BF16)
    p1 = logf.astype(BF16)
    r1 = logf - p1.astype(F32)
    p2 = r1.astype(BF16)
    p3 = (r1 - p2.astype(F32)).astype(BF16)
    cs = (jnp.dot(tri, p1, preferred_element_type=F32)
          + jnp.dot(tri, p2, preferred_element_type=F32)
          + jnp.dot(tri, p3, preferred_element_type=F32))

    @pl.when(pl.program_id(1) == 0)
    def _():
        carry_ref[...] = jnp.zeros_like(carry_ref)

    fc = cs + carry_ref[...]
    carry_ref[...] = fc[ts - 1:ts, :]
    fc2 = fc * LOG2E
    frow_ref[0, 0] = fc2.T[:N_HEADS, :]
    for h in range(N_HEADS):
        frep_ref[h, 0] = jnp.broadcast_to(fc2[:, h:h + 1], (ts, LANES))


def _lat_call(x, shift, scale, wlat, qg, kvg, wuq, wukv, cc, ss, fb):
    b, s, d = x.shape
    ts = LAT_ROWS
    const = lambda shape: pl.BlockSpec(shape, lambda i, j: (0,) * len(shape))
    tok = lambda w: pl.BlockSpec((1, ts, w), lambda i, j: (i, j, 0))
    per_batch = pl.BlockSpec((1, 1, d), lambda i, j: (i, 0, 0))
    head = lambda w: pl.BlockSpec((N_HEADS, 1, ts, w), lambda i, j: (0, i, j, 0))
    return pl.pallas_call(
        _lat_kernel,
        grid=(b, s // ts),
        in_specs=[tok(d), per_batch, per_batch, const((d, LAT_W)), const((1, Q_LORA)), const((1, KV_LORA)),
                  const((Q_LORA, N_HEADS * MLA_QK)), const((KV_LORA, N_HEADS * 2 * HEAD_DIM)),
                  tok(LANES), tok(LANES), const((1, LANES))],
        out_specs=[tok(d), head(MLA_QK), head(MLA_QK), head(HEAD_DIM), head(LANES),
                   pl.BlockSpec((1, 1, N_HEADS, ts), lambda i, j: (i, j, 0, 0))],
        out_shape=[jax.ShapeDtypeStruct((b, s, d), BF16),
                   jax.ShapeDtypeStruct((N_HEADS, b, s, MLA_QK), BF16),
                   jax.ShapeDtypeStruct((N_HEADS, b, s, MLA_QK), BF16),
                   jax.ShapeDtypeStruct((N_HEADS, b, s, HEAD_DIM), BF16),
                   jax.ShapeDtypeStruct((N_HEADS, b, s, LANES), F32),
                   jax.ShapeDtypeStruct((b, s // ts, N_HEADS, ts), F32)],
        scratch_shapes=[pltpu.VMEM((1, LANES), F32)],
        compiler_params=_params("arbitrary", "arbitrary"),
        name="ln_latent",
    )(x, shift, scale, wlat, qg, kvg, wuq, wukv, cc, ss, fb)


def _proj_kernel(u_ref, w_ref, o_ref):
    acc = jnp.dot(u_ref[...], w_ref[...], preferred_element_type=F32)
    for g in range(o_ref.shape[0]):
        o_ref[g] = acc[:, g * LANES:(g + 1) * LANES].astype(BF16)


def _proj_call(u, w):
    n, d = u.shape
    ncol = w.shape[1]
    tm, tn = 1024, 1024
    return pl.pallas_call(
        _proj_kernel,
        grid=(n // tm, ncol // tn),
        in_specs=[pl.BlockSpec((tm, d), lambda i, j: (i, 0)),
                  pl.BlockSpec((d, tn), lambda i, j: (0, j))],
        out_specs=pl.BlockSpec((tn // LANES, tm, LANES), lambda i, j: (j, i, 0)),
        out_shape=jax.ShapeDtypeStruct((ncol // LANES, n, LANES), BF16),
        compiler_params=_params("arbitrary", "arbitrary"),
        name="in_proj",
    )(u, w)


def _block_iotas(t):
    row = lax.broadcasted_iota(jnp.int32, (t, t), 0)
    col = lax.broadcasted_iota(jnp.int32, (t, t), 1)
    return row, col


def _kq(k, q):
    return lax.dot_general(k, q, (((1,), (1,)), ((), ())), preferred_element_type=F32)


def _stage_values(v_ref, vt_ref, ones_rows):
    n_blk, t = vt_ref.shape[1], vt_ref.shape[3]

    @pl.when(pl.program_id(1) == 0)
    def _():
        def per_head(h, carry):
            for j in range(n_blk):
                vt = v_ref[h, 0, j * t:(j + 1) * t, :].astype(F32).T
                vt_ref[h, j, 0:HEAD_DIM, :] = vt.astype(BF16)
                if ones_rows:
                    vt_ref[h, j, HEAD_DIM:HEAD_DIM + ones_rows, :] = jnp.ones((ones_rows, t), BF16)
            return carry
        lax.fori_loop(0, N_HEADS, per_head, 0)


SUM_ROWS = 8


def _softmax_kernel(*refs, forget):
    if forget:
        q_ref, k_ref, v_ref, frep_ref, frow_ref, o_ref, vt_ref, m_ref, acc_ref, s_ref = refs
    else:
        q_ref, k_ref, v_ref, o_ref, vt_ref, m_ref, acc_ref, s_ref = refs
    t = q_ref.shape[2]
    qi = pl.program_id(1)
    key, qry = _block_iotas(t)
    mask = (key <= qry) if forget else ((key // CHUNK) <= (qry // CHUNK))

    _stage_values(v_ref, vt_ref, SUM_ROWS)
    m_ref[...] = jnp.full(m_ref.shape, NEG_INF, F32)
    acc_ref[...] = jnp.zeros(acc_ref.shape, F32)

    def scores(j, h):
        off = pl.multiple_of(j * t, t)
        s_ref[h] = _kq(k_ref[h, 0, pl.ds(off, t), :], q_ref[h, 0])

    def consume(j, h, diagonal):
        s = s_ref[h]
        if forget:
            off = pl.multiple_of(j * t, t)
            fk = frep_ref[h, 0, pl.ds(off, t), :]
            s = s + (frow_ref[0, qi, h:h + 1, :] - jnp.concatenate([fk] * (t // LANES), axis=1))
        if diagonal:
            s = jnp.where(mask, s, NEG_INF)
        m_old = m_ref[h]
        m_new = jnp.maximum(m_old, jnp.max(s, axis=0, keepdims=True))
        p = jnp.exp2(s - m_new).astype(BF16)
        acc_ref[h] = (jnp.exp2(m_old - m_new) * acc_ref[h]
                      + jnp.dot(vt_ref[h, j], p, preferred_element_type=F32))
        m_ref[h] = m_new

    for h in range(N_HEADS):
        scores(0, h)

    def body(j, carry):
        for h in range(N_HEADS):
            consume(j, h, False)
            scores(j + 1, h)
        return carry

    lax.fori_loop(0, qi, body, 0)
    for h in range(N_HEADS):
        consume(qi, h, True)
        acc = acc_ref[h]
        o_ref[h, 0] = (acc[:HEAD_DIM] / acc[HEAD_DIM:HEAD_DIM + 1]).T.astype(BF16)


def _stick_kernel(q_ref, k_ref, v_ref, o_ref, vt_ref, c_ref, acc_ref, s_ref, lb_ref, loc_ref, first_ref):
    t = q_ref.shape[2]
    qi = pl.program_id(1)
    key, qry = _block_iotas(t)
    strict = key < qry
    later = jnp.where(qry > key, 1.0, 0.0).astype(BF16)

    _stage_values(v_ref, vt_ref, 0)
    c_ref[...] = jnp.zeros(c_ref.shape, F32)
    acc_ref[...] = jnp.zeros(acc_ref.shape, F32)

    def scores(j, h):
        off = pl.multiple_of(j * t, t)
        s_ref[h] = _kq(k_ref[h, 0, pl.ds(off, t), :], q_ref[h, 0])

    def decay(h, diagonal):
        z = s_ref[h]
        log_beta = jnp.minimum(z, 0.0) - jnp.log2(1.0 + jnp.exp2(-jnp.abs(z)))
        log_1mb = log_beta - z
        if diagonal:
            log_1mb = jnp.where(strict, log_1mb, 0.0)
        hi = log_1mb.astype(BF16)
        lo = (log_1mb - hi.astype(F32)).astype(BF16)
        lb_ref[h] = log_beta
        first_ref[h] = log_1mb[0:1, :]
        loc_ref[h] = (jnp.dot(later, hi, preferred_element_type=F32)
                      + jnp.dot(later, lo, preferred_element_type=F32))

    def weigh(j, h, diagonal):
        local = loc_ref[h]
        c = c_ref[h]
        a = jnp.exp2(lb_ref[h] + (local + c))
        if diagonal:
            a = jnp.where(strict, a, 0.0)
        acc_ref[h] = acc_ref[h] + jnp.dot(vt_ref[h, j], a.astype(BF16), preferred_element_type=F32)
        c_ref[h] = c + (local[0:1, :] + first_ref[h])

    def step(j, diagonal, prefetch):
        for h in range(N_HEADS):
            decay(h, diagonal)
            if prefetch:
                scores(j - 1, h)
        for h in range(N_HEADS):
            weigh(j, h, diagonal)

    for h in range(N_HEADS):
        scores(qi, h)

    @pl.when(qi == 0)
    def _():
        step(0, True, False)

    @pl.when(qi > 0)
    def _():
        step(qi, True, True)

        def body(i, carry):
            step(qi - i, False, True)
            return carry

        lax.fori_loop(1, qi, body, 0)
        step(0, False, False)

    for h in range(N_HEADS):
        o_ref[h, 0] = acc_ref[h].T.astype(BF16)


def _attn_call(kernel, name, q, k, v, q_blk, k_blk, v_blk, extra=(), extra_specs=(), softmax=True):
    _, b, s, _ = q.shape
    t = ATT_BLOCK
    qw, kw, vw = q.shape[3], k.shape[3], v.shape[3]
    rows = HEAD_DIM + (SUM_ROWS if softmax else 0)
    vt = pltpu.VMEM((N_HEADS, s // t, rows, t), BF16)
    stat = pltpu.VMEM((N_HEADS, 1, t), F32)
    acc = pltpu.VMEM((N_HEADS, rows, t), F32)
    tile = pltpu.VMEM((N_HEADS, t, t), F32)
    return pl.pallas_call(
        kernel,
        grid=(b, s // t),
        in_specs=[pl.BlockSpec((N_HEADS, 1, t, qw), lambda i, j: (q_blk, i, j, 0)),
                  pl.BlockSpec((N_HEADS, 1, s, kw), lambda i, j: (k_blk, i, 0, 0)),
                  pl.BlockSpec((N_HEADS, 1, s, vw), lambda i, j: (v_blk, i, 0, 0)),
                  *extra_specs],
        out_specs=pl.BlockSpec((N_HEADS, 1, t, HEAD_DIM), lambda i, j: (0, i, j, 0)),
        out_shape=jax.ShapeDtypeStruct((N_HEADS, b, s, HEAD_DIM), BF16),
        scratch_shapes=[vt, stat, acc, tile] if softmax else [vt, stat, acc, tile, tile, tile, stat],
        compiler_params=_params("arbitrary", "arbitrary"),
        name=name,
    )(q, k, v, *extra)


def _merge_kernel(oa_ref, ob_ref, oc_ref, ga_ref, gb_ref, gc_ref, ma_ref, mb_ref, mc_ref, wb_ref,
                  out_ref, ys_ref):
    @pl.when(pl.program_id(1) == 0)
    def _():
        for n, (o_ref, g_ref) in enumerate(((oa_ref, ga_ref), (ob_ref, gb_ref), (oc_ref, gc_ref))):
            for h in range(N_HEADS):
                g = g_ref[h].astype(F32)
                ys_ref[n, :, h * HEAD_DIM:(h + 1) * HEAD_DIM] = (
                    o_ref[h].astype(F32) * (g * jax.nn.sigmoid(g))).astype(BF16)

    branches = [jnp.dot(ys_ref[n], wb_ref[n], preferred_element_type=F32) for n in range(N_BRANCH)]
    for g in range(ma_ref.shape[0]):
        sl = slice(g * LANES, (g + 1) * LANES)
        acc = jax.nn.sigmoid(ma_ref[g].astype(F32)) * branches[0][:, sl]
        acc = acc + jax.nn.sigmoid(mb_ref[g].astype(F32)) * branches[1][:, sl]
        acc = acc + jax.nn.sigmoid(mc_ref[g].astype(F32)) * branches[2][:, sl]
        out_ref[:, sl] = acc.astype(BF16)


def _merge_call(o_a, o_b, o_c, proj, wb, blk):
    n = proj.shape[1]
    d = wb.shape[2]
    tm, tn = 1024, 256
    gpt = tn // LANES
    per_branch = d // tn
    heads = lambda c: pl.BlockSpec((N_HEADS, tm, LANES), lambda i, j: (c, i, 0))
    gate = lambda b: pl.BlockSpec((gpt, tm, LANES), lambda i, j: (b * per_branch + j, i, 0))
    return pl.pallas_call(
        _merge_kernel,
        grid=(n // tm, d // tn),
        in_specs=[heads(0), heads(0), heads(0), heads(blk["g_a"]), heads(blk["g_b"]), heads(blk["g_c"]),
                  gate(0), gate(1), gate(2),
                  pl.BlockSpec((N_BRANCH, BRANCH_W, tn), lambda i, j: (0, 0, j))],
        out_specs=pl.BlockSpec((tm, tn), lambda i, j: (i, j)),
        out_shape=jax.ShapeDtypeStruct((n, d), BF16),
        scratch_shapes=[pltpu.VMEM((N_BRANCH, tm, BRANCH_W), BF16)],
        compiler_params=_params("arbitrary", "arbitrary"),
        name="gate_branch_merge",
    )(o_a, o_b, o_c, proj, proj, proj, proj, proj, proj, wb)


def _out_kernel(m_ref, w_ref, x_ref, gate_ref, g_ref, b_ref, o_ref, *, alpha):
    out = jnp.dot(m_ref[...], w_ref[...], preferred_element_type=F32)
    r = alpha * x_ref[0] + gate_ref[0] * out
    mu = jnp.mean(r, axis=-1, keepdims=True)
    rc = r - mu
    var = jnp.mean(rc * rc, axis=-1, keepdims=True)
    o_ref[0] = rc * lax.rsqrt(var + LN_EPS) * g_ref[...] + b_ref[...]


def _out_call(merged, w_out, x, gate, ln_g, ln_b, alpha):
    b, s, d = x.shape
    tm = 512
    per_seq = s // tm
    return pl.pallas_call(
        functools.partial(_out_kernel, alpha=alpha),
        grid=(b, per_seq),
        in_specs=[pl.BlockSpec((tm, d), lambda i, j: (i * per_seq + j, 0)),
                  pl.BlockSpec((d, d), lambda i, j: (0, 0)),
                  pl.BlockSpec((1, tm, d), lambda i, j: (i, j, 0)),
                  pl.BlockSpec((1, 1, d), lambda i, j: (i, 0, 0)),
                  pl.BlockSpec((1, d), lambda i, j: (0, 0)),
                  pl.BlockSpec((1, d), lambda i, j: (0, 0))],
        out_specs=pl.BlockSpec((1, tm, d), lambda i, j: (i, j, 0)),
        out_shape=jax.ShapeDtypeStruct((b, s, d), F32),
        compiler_params=_params("arbitrary", "arbitrary"),
        name="out_proj_ln",
    )(merged, w_out, x, gate, ln_g, ln_b)


def _swap_halves(w):
    half = w.shape[-1] // 2
    return jnp.concatenate([w[..., half:], w[..., :half]], axis=-1)


def _prepare_weights(w_in, w_uq, w_ukv, w_branch, w_out):
    depth, d, _ = w_in.shape
    sizes = (Q_LORA, KV_LORA, MLA_ROPE, BRANCH_W, 3 * BRANCH_W, N_HEADS, BRANCH_W, 3 * BRANCH_W, BRANCH_W,
             N_BRANCH * d)
    parts, off = [], 0
    for size in sizes:
        parts.append(w_in[:, :, off:off + size])
        off += size
    c_q, c_kv, k_rope, g_a, qkv_b, f_logit, g_b, qkv_c, g_c, merge = parts
    pad = jnp.zeros((depth, d, LANES - N_HEADS), w_in.dtype)
    w_lat = jnp.concatenate([c_q, c_kv, k_rope, _swap_halves(k_rope), f_logit, pad], axis=-1).astype(BF16)
    s = LOG2E / math.sqrt(HEAD_DIM)
    cast = lambda w: w.astype(BF16)
    w_main = jnp.concatenate([cast(merge), cast(g_a), cast(qkv_b[..., :BRANCH_W] * s), cast(qkv_b[..., BRANCH_W:]),
                              cast(g_b), cast(qkv_c[..., :BRANCH_W] * s), cast(qkv_c[..., BRANCH_W:]), cast(g_c)],
                             axis=-1)
    uq = w_uq.reshape(depth, Q_LORA, N_HEADS, MLA_NOPE + MLA_ROPE)
    rope_cols = uq[..., MLA_NOPE:]
    uq = jnp.concatenate([uq[..., :MLA_NOPE], rope_cols, _swap_halves(rope_cols)], axis=-1)
    uq = uq.reshape(depth, Q_LORA, N_HEADS * MLA_QK).astype(BF16)
    return w_lat, w_main, uq, w_ukv.astype(BF16), w_branch.astype(BF16), w_out.astype(BF16)


def _proj_blocks(d):
    first = N_BRANCH * d // (N_HEADS * LANES)
    names = ("g_a", "q_b", "k_b", "v_b", "g_b", "q_c", "k_c", "v_c", "g_c")
    return {name: first + i for i, name in enumerate(names)}


def kernel(x, c, positions, w_ada, b_ada, w_in, q_norm_g, kv_norm_g, w_uq, w_ukv, fox_bias, w_branch, w_out,
           ln_g, ln_b):
    b, s, d = x.shape
    depth = w_in.shape[0]
    alpha = (2 * depth) ** 0.25
    blk = _proj_blocks(d)

    w_lat, w_main, uq, ukv, wb, wo = _prepare_weights(w_in, w_uq, w_ukv, w_branch, w_out)

    c_pad = jnp.zeros((8, d), BF16).at[:b].set(c.astype(BF16))
    mod = _ada_call(c_pad, w_ada, b_ada)[:, :b].reshape(depth, b, 3, 1, d)

    half = MLA_ROPE // 2
    inv_freq = ROPE_THETA ** (-jnp.arange(half, dtype=F32) / half)
    invf = jnp.concatenate([inv_freq, inv_freq, jnp.zeros((LANES - MLA_ROPE,), F32)]).reshape(1, LANES)
    cc, ss = _rope_call(positions, invf)

    fb = jnp.zeros((depth, 1, LANES), F32).at[:, 0, :N_HEADS].set(fox_bias)

    for l in range(depth):
        shift, scale, gate = mod[l, :, 0], mod[l, :, 1], mod[l, :, 2]
        u, q_a, k_a, v_a, frep, frow = _lat_call(
            x, shift, scale, w_lat[l], q_norm_g[l].reshape(1, -1), kv_norm_g[l].reshape(1, -1),
            uq[l], ukv[l], cc, ss, fb[l])
        proj = _proj_call(u.reshape(b * s, d), w_main[l])
        proj4 = proj.reshape(proj.shape[0], b, s, LANES)

        o_a = _attn_call(functools.partial(_softmax_kernel, forget=False), "mla_attention",
                         q_a, k_a, v_a, 0, 0, 0)
        t = ATT_BLOCK
        o_b = _attn_call(functools.partial(_softmax_kernel, forget=True), "fox_attention",
                         proj4, proj4, proj4, blk["q_b"], blk["k_b"], blk["v_b"],
                         extra=(frep, frow),
                         extra_specs=(pl.BlockSpec((N_HEADS, 1, s, LANES), lambda i, j: (0, i, 0, 0)),
                                      pl.BlockSpec((1, s // t, N_HEADS, t), lambda i, j: (i, 0, 0, 0))))
        o_c = _attn_call(_stick_kernel, "stick_attention",
                         proj4, proj4, proj4, blk["q_c"], blk["k_c"], blk["v_c"], softmax=False)

        merged = _merge_call(o_a.reshape(N_HEADS, b * s, LANES), o_b.reshape(N_HEADS, b * s, LANES),
                             o_c.reshape(N_HEADS, b * s, LANES), proj, wb[l], blk)
        x = _out_call(merged, wo[l], x, gate, ln_g[l].reshape(1, d), ln_b[l].reshape(1, d), alpha)
    return x
```

```python
import functools
import math

import jax
import jax.numpy as jnp
from jax import lax
from jax.experimental import pallas as pl
from jax.experimental.pallas import tpu as pltpu

F32 = jnp.float32
BF16 = jnp.bfloat16

N_HEADS = 8
HEAD_DIM = 128
MLA_NOPE = 128
MLA_ROPE = 64
MLA_QK = 2 * HEAD_DIM
Q_LORA = 512
KV_LORA = 256
CHUNK = 64
N_BRANCH = 3
BRANCH_W = N_HEADS * HEAD_DIM
ROPE_THETA = 10000.0
LN_EPS = 1e-5
RMS_EPS = 1e-6
NEG_INF = -1e30
LOG2E = math.log2(math.e)
LANES = 128
LAT_W = 1024
VMEM_LIMIT = 52 * 1024 * 1024

ATT_BLOCK = 256
LAT_ROWS = 256


def _params(*sem):
    return pltpu.CompilerParams(dimension_semantics=sem, vmem_limit_bytes=VMEM_LIMIT)


def _log_sigmoid(x):
    return jnp.minimum(x, 0.0) - jnp.log1p(jnp.exp(-jnp.abs(x)))


def _ada_kernel(c_ref, w_ref, b_ref, o_ref):
    w = w_ref[0].astype(BF16)
    o_ref[0] = jnp.dot(c_ref[...], w, preferred_element_type=F32) + b_ref[0]


def _ada_call(c_pad, w_ada, b_ada):
    depth, d, n3 = w_ada.shape
    tn = 1024
    return pl.pallas_call(
        _ada_kernel,
        grid=(depth, n3 // tn),
        in_specs=[pl.BlockSpec((8, d), lambda l, j: (0, 0)),
                  pl.BlockSpec((1, d, tn), lambda l, j: (l, 0, j)),
                  pl.BlockSpec((1, 1, tn), lambda l, j: (l, 0, j))],
        out_specs=pl.BlockSpec((1, 8, tn), lambda l, j: (l, 0, j)),
        out_shape=jax.ShapeDtypeStruct((depth, 8, n3), F32),
        compiler_params=_params("arbitrary", "arbitrary"),
        name="ada_mod",
    )(c_pad, w_ada, b_ada.reshape(depth, 1, n3))


def _rope_kernel(pos_ref, invf_ref, cc_ref, ss_ref):
    ang = pos_ref[0].astype(F32) * invf_ref[...]
    lane = lax.broadcasted_iota(jnp.int32, ang.shape, 1)
    cos = jnp.cos(ang)
    sin = jnp.sin(ang)
    half = MLA_ROPE // 2
    cc_ref[0] = jnp.where(lane < MLA_ROPE, cos, 0.0)
    ss_ref[0] = jnp.where(lane < half, -sin, jnp.where(lane < MLA_ROPE, sin, 0.0))


def _rope_call(positions, invf):
    b, s = positions.shape
    ts = 512
    spec = pl.BlockSpec((1, ts, LANES), lambda i, j: (i, j, 0))
    return pl.pallas_call(
        _rope_kernel,
        grid=(b, s // ts),
        in_specs=[pl.BlockSpec((1, ts, 1), lambda i, j: (i, j, 0)),
                  pl.BlockSpec((1, LANES), lambda i, j: (0, 0))],
        out_specs=[spec, spec],
        out_shape=[jax.ShapeDtypeStruct((b, s, LANES), F32)] * 2,
        compiler_params=_params("arbitrary", "arbitrary"),
        name="rope_tables",
    )(positions.reshape(b, s, 1), invf)


def _rope_apply(r, cc, ss):
    return r * cc + pltpu.roll(r, MLA_ROPE, 1) * ss


def _lat_kernel(x_ref, shift_ref, scale_ref, wlat_ref, qg_ref, kvg_ref, wuq_ref, wukv_ref,
                cc_ref, ss_ref, fb_ref,
                u_ref, q_ref, k_ref, v_ref, frep_ref, frow_ref, carry_ref):
    ts = x_ref.shape[1]
    x = x_ref[0]
    mu = jnp.mean(x, axis=-1, keepdims=True)
    xc = x - mu
    var = jnp.mean(xc * xc, axis=-1, keepdims=True)
    y = xc * lax.rsqrt(var + LN_EPS)
    u = (y * (1.0 + scale_ref[0]) + shift_ref[0]).astype(BF16)
    u_ref[0] = u

    lat = jnp.dot(u, wlat_ref[...], preferred_element_type=F32)
    cc = cc_ref[0]
    ss = ss_ref[0]
    q_scale = LOG2E / math.sqrt(MLA_NOPE + MLA_ROPE)

    cq = lat[:, :Q_LORA]
    cqn = cq * lax.rsqrt(jnp.mean(cq * cq, axis=-1, keepdims=True) + RMS_EPS) * qg_ref[...]
    q = jnp.dot(cqn.astype(BF16), wuq_ref[...], preferred_element_type=F32)
    for h in range(N_HEADS):
        base = h * MLA_QK
        q_ref[h, 0, :, 0:HEAD_DIM] = (q[:, base:base + HEAD_DIM] * q_scale).astype(BF16)
        r = q[:, base + HEAD_DIM:base + MLA_QK]
        q_ref[h, 0, :, HEAD_DIM:MLA_QK] = (_rope_apply(r, cc, ss) * q_scale).astype(BF16)

    ckv = lat[:, Q_LORA:Q_LORA + KV_LORA]
    ckvn = ckv * lax.rsqrt(jnp.mean(ckv * ckv, axis=-1, keepdims=True) + RMS_EPS) * kvg_ref[...]
    kv = jnp.dot(ckvn.astype(BF16), wukv_ref[...], preferred_element_type=F32)
    r_off = Q_LORA + KV_LORA
    kpe = _rope_apply(lat[:, r_off:r_off + LANES], cc, ss).astype(BF16)
    for h in range(N_HEADS):
        base = h * 2 * HEAD_DIM
        k_ref[h, 0, :, 0:HEAD_DIM] = kv[:, base:base + HEAD_DIM].astype(BF16)
        k_ref[h, 0, :, HEAD_DIM:MLA_QK] = kpe
        v_ref[h, 0] = kv[:, base + HEAD_DIM:base + 2 * HEAD_DIM].astype(BF16)

    f_off = r_off + LANES
    logf = _log_sigmoid(lat[:, f_off:f_off + LANES] + fb_ref[...])
    lane = lax.broadcasted_iota(jnp.int32, logf.shape, 1)
    logf = jnp.where(lane < N_HEADS, logf, 0.0)
    row = lax.broadcasted_iota(jnp.int32, (ts, ts), 0)
    col = lax.broadcasted_iota(jnp.int32, (ts, ts), 1)
    tri = jnp.where(row >= col, 1.0, 0.0).astype(BF16)
    p1 = logf.astype(BF16)
    r1 = logf - p1.astype(F32)
    p2 = r1.astype(BF16)
    p3 = (r1 - p2.astype(F32)).astype(BF16)
    cs = (jnp.dot(tri, p1, preferred_element_type=F32)
          + jnp.dot(tri, p2, preferred_element_type=F32)
          + jnp.dot(tri, p3, preferred_element_type=F32))

    @pl.when(pl.program_id(1) == 0)
    def _():
        carry_ref[...] = jnp.zeros_like(carry_ref)

    fc = cs + carry_ref[...]
    carry_ref[...] = fc[ts - 1:ts, :]
    fc2 = fc * LOG2E
    frow_ref[0, 0] = fc2.T[:N_HEADS, :]
    for h in range(N_HEADS):
        frep_ref[h, 0] = jnp.broadcast_to(fc2[:, h:h + 1], (ts, LANES))


def _lat_call(x, shift, scale, wlat, qg, kvg, wuq, wukv, cc, ss, fb):
    b, s, d = x.shape
    ts = LAT_ROWS
    const = lambda shape: pl.BlockSpec(shape, lambda i, j: (0,) * len(shape))
    tok = lambda w: pl.BlockSpec((1, ts, w), lambda i, j: (i, j, 0))
    per_batch = pl.BlockSpec((1, 1, d), lambda i, j: (i, 0, 0))
    head = lambda w: pl.BlockSpec((N_HEADS, 1, ts, w), lambda i, j: (0, i, j, 0))
    return pl.pallas_call(
        _lat_kernel,
        grid=(b, s // ts),
        in_specs=[tok(d), per_batch, per_batch, const((d, LAT_W)), const((1, Q_LORA)), const((1, KV_LORA)),
                  const((Q_LORA, N_HEADS * MLA_QK)), const((KV_LORA, N_HEADS * 2 * HEAD_DIM)),
                  tok(LANES), tok(LANES), const((1, LANES))],
        out_specs=[tok(d), head(MLA_QK), head(MLA_QK), head(HEAD_DIM), head(LANES),
                   pl.BlockSpec((1, 1, N_HEADS, ts), lambda i, j: (i, j, 0, 0))],
        out_shape=[jax.ShapeDtypeStruct((b, s, d), BF16),
                   jax.ShapeDtypeStruct((N_HEADS, b, s, MLA_QK), BF16),
                   jax.ShapeDtypeStruct((N_HEADS, b, s, MLA_QK), BF16),
                   jax.ShapeDtypeStruct((N_HEADS, b, s, HEAD_DIM), BF16),
                   jax.ShapeDtypeStruct((N_HEADS, b, s, LANES), F32),
                   jax.ShapeDtypeStruct((b, s // ts, N_HEADS, ts), F32)],
        scratch_shapes=[pltpu.VMEM((1, LANES), F32)],
        compiler_params=_params("arbitrary", "arbitrary"),
        name="ln_latent",
    )(x, shift, scale, wlat, qg, kvg, wuq, wukv, cc, ss, fb)


def _proj_kernel(u_ref, w_ref, o_ref):
    acc = jnp.dot(u_ref[...], w_ref[...], preferred_element_type=F32)
    for g in range(o_ref.shape[0]):
        o_ref[g] = acc[:, g * LANES:(g + 1) * LANES].astype(BF16)


def _proj_call(u, w):
    n, d = u.shape
    ncol = w.shape[1]
    tm, tn = 1024, 1024
    return pl.pallas_call(
        _proj_kernel,
        grid=(n // tm, ncol // tn),
        in_specs=[pl.BlockSpec((tm, d), lambda i, j: (i, 0)),
                  pl.BlockSpec((d, tn), lambda i, j: (0, j))],
        out_specs=pl.BlockSpec((tn // LANES, tm, LANES), lambda i, j: (j, i, 0)),
        out_shape=jax.ShapeDtypeStruct((ncol // LANES, n, LANES), BF16),
        compiler_params=_params("arbitrary", "arbitrary"),
        name="in_proj",
    )(u, w)


def _block_iotas(t):
    row = lax.broadcasted_iota(jnp.int32, (t, t), 0)
    col = lax.broadcasted_iota(jnp.int32, (t, t), 1)
    return row, col


def _kq(k, q):
    return lax.dot_general(k, q, (((1,), (1,)), ((), ())), preferred_element_type=F32)


def _stage_values(v_ref, vt_ref, ones_rows):
    n_blk, t = vt_ref.shape[1], vt_ref.shape[3]

    @pl.when(pl.program_id(1) == 0)
    def _():
        def per_head(h, carry):
            for j in range(n_blk):
                vt = v_ref[h, 0, j * t:(j + 1) * t, :].astype(F32).T
                vt_ref[h, j, 0:HEAD_DIM, :] = vt.astype(BF16)
                if ones_rows:
                    vt_ref[h, j, HEAD_DIM:HEAD_DIM + ones_rows, :] = jnp.ones((ones_rows, t), BF16)
            return carry
        lax.fori_loop(0, N_HEADS, per_head, 0)


SUM_ROWS = 8
DEAD_LOG2 = -150.0


def _softmax_kernel(*refs, forget):
    if forget:
        q_ref, k_ref, v_ref, frep_ref, frow_ref, o_ref, vt_ref, m_ref, acc_ref, s_ref = refs
    else:
        q_ref, k_ref, v_ref, o_ref, vt_ref, m_ref, acc_ref, s_ref = refs
    t = q_ref.shape[2]
    qi = pl.program_id(1)
    key, qry = _block_iotas(t)
    mask = (key <= qry) if forget else ((key // CHUNK) <= (qry // CHUNK))

    _stage_values(v_ref, vt_ref, SUM_ROWS)
    m_ref[...] = jnp.full(m_ref.shape, NEG_INF, F32)
    acc_ref[...] = jnp.zeros(acc_ref.shape, F32)

    def scores(j, h):
        off = pl.multiple_of(j * t, t)
        s_ref[h] = _kq(k_ref[h, 0, pl.ds(off, t), :], q_ref[h, 0])

    def consume(j, h, diagonal):
        s = s_ref[h]
        if forget:
            off = pl.multiple_of(j * t, t)
            fk = frep_ref[h, 0, pl.ds(off, t), :]
            s = s + (frow_ref[0, qi, h:h + 1, :] - jnp.concatenate([fk] * (t // LANES), axis=1))
        if diagonal:
            s = jnp.where(mask, s, NEG_INF)
        m_old = m_ref[h]
        m_new = jnp.maximum(m_old, jnp.max(s, axis=0, keepdims=True))
        p = jnp.exp2(s - m_new).astype(BF16)
        acc_ref[h] = (jnp.exp2(m_old - m_new) * acc_ref[h]
                      + jnp.dot(vt_ref[h, j], p, preferred_element_type=F32))
        m_ref[h] = m_new

    for h in range(N_HEADS):
        scores(0, h)

    def body(j, carry):
        for h in range(N_HEADS):
            consume(j, h, False)
            scores(j + 1, h)
        return carry

    lax.fori_loop(0, qi, body, 0)
    for h in range(N_HEADS):
        consume(qi, h, True)
        acc = acc_ref[h]
        o_ref[h, 0] = (acc[:HEAD_DIM] / acc[HEAD_DIM:HEAD_DIM + 1]).T.astype(BF16)


def _stick_kernel(q_ref, k_ref, v_ref, o_ref, vt_ref, c_ref, acc_ref, s_ref, lb_ref, loc_ref, first_ref):
    t = q_ref.shape[2]
    qi = pl.program_id(1)
    key, qry = _block_iotas(t)
    strict = key < qry
    later = jnp.where(qry > key, 1.0, 0.0).astype(BF16)

    _stage_values(v_ref, vt_ref, 0)
    c_ref[...] = jnp.zeros(c_ref.shape, F32)
    acc_ref[...] = jnp.zeros(acc_ref.shape, F32)

    def scores(j, h):
        off = pl.multiple_of(j * t, t)
        s_ref[h] = _kq(k_ref[h, 0, pl.ds(off, t), :], q_ref[h, 0])

    def decay(h, diagonal):
        z = s_ref[h]
        log_beta = jnp.minimum(z, 0.0) - jnp.log2(1.0 + jnp.exp2(-jnp.abs(z)))
        log_1mb = log_beta - z
        if diagonal:
            log_1mb = jnp.where(strict, log_1mb, 0.0)
        hi = log_1mb.astype(BF16)
        lo = (log_1mb - hi.astype(F32)).astype(BF16)
        lb_ref[h] = log_beta
        first_ref[h] = log_1mb[0:1, :]
        loc_ref[h] = (jnp.dot(later, hi, preferred_element_type=F32)
                      + jnp.dot(later, lo, preferred_element_type=F32))

    def weigh(j, h, diagonal):
        local = loc_ref[h]
        c = c_ref[h]
        a = jnp.exp2(lb_ref[h] + (local + c))
        if diagonal:
            a = jnp.where(strict, a, 0.0)
        acc_ref[h] = acc_ref[h] + jnp.dot(vt_ref[h, j], a.astype(BF16), preferred_element_type=F32)
        c_ref[h] = c + (local[0:1, :] + first_ref[h])

    def step(j, diagonal, prefetch):
        for h in range(N_HEADS):
            decay(h, diagonal)
            if prefetch is not None:
                scores(prefetch, h)
        for h in range(N_HEADS):
            weigh(j, h, diagonal)

    for h in range(N_HEADS):
        scores(qi, h)

    @pl.when(qi == 0)
    def _():
        step(0, True, None)

    def any_weight_left():
        c_max = jnp.max(jnp.max(c_ref[...], axis=0), axis=1, keepdims=True)
        return c_max[0, 0] > DEAD_LOG2

    @pl.when(qi > 0)
    def _():
        step(qi, True, qi - 1)

        def body(state):
            j, _ = state
            step(j, False, jnp.maximum(j - 1, 0))
            return j - 1, any_weight_left()

        lax.while_loop(lambda state: jnp.logical_and(state[0] >= 0, state[1]), body,
                       (qi - 1, any_weight_left()))

    for h in range(N_HEADS):
        o_ref[h, 0] = acc_ref[h].T.astype(BF16)


def _attn_call(kernel, name, q, k, v, q_blk, k_blk, v_blk, extra=(), extra_specs=(), softmax=True):
    _, b, s, _ = q.shape
    t = ATT_BLOCK
    qw, kw, vw = q.shape[3], k.shape[3], v.shape[3]
    rows = HEAD_DIM + (SUM_ROWS if softmax else 0)
    vt = pltpu.VMEM((N_HEADS, s // t, rows, t), BF16)
    stat = pltpu.VMEM((N_HEADS, 1, t), F32)
    acc = pltpu.VMEM((N_HEADS, rows, t), F32)
    tile = pltpu.VMEM((N_HEADS, t, t), F32)
    return pl.pallas_call(
        kernel,
        grid=(b, s // t),
        in_specs=[pl.BlockSpec((N_HEADS, 1, t, qw), lambda i, j: (q_blk, i, j, 0)),
                  pl.BlockSpec((N_HEADS, 1, s, kw), lambda i, j: (k_blk, i, 0, 0)),
                  pl.BlockSpec((N_HEADS, 1, s, vw), lambda i, j: (v_blk, i, 0, 0)),
                  *extra_specs],
        out_specs=pl.BlockSpec((N_HEADS, 1, t, HEAD_DIM), lambda i, j: (0, i, j, 0)),
        out_shape=jax.ShapeDtypeStruct((N_HEADS, b, s, HEAD_DIM), BF16),
        scratch_shapes=[vt, stat, acc, tile] if softmax else [vt, stat, acc, tile, tile, tile, stat],
        compiler_params=_params("arbitrary", "arbitrary"),
        name=name,
    )(q, k, v, *extra)


def _merge_kernel(oa_ref, ob_ref, oc_ref, ga_ref, gb_ref, gc_ref, ma_ref, mb_ref, mc_ref, wb_ref,
                  out_ref, ys_ref):
    for n, (o_ref, g_ref) in enumerate(((oa_ref, ga_ref), (ob_ref, gb_ref), (oc_ref, gc_ref))):
        for h in range(N_HEADS):
            g = g_ref[h].astype(F32)
            ys_ref[n, :, h * HEAD_DIM:(h + 1) * HEAD_DIM] = (
                o_ref[h].astype(F32) * (g * jax.nn.sigmoid(g))).astype(BF16)

    gates = (ma_ref, mb_ref, mc_ref)
    for c0 in range(0, out_ref.shape[1], MERGE_CHUNK):
        branches = [jnp.dot(ys_ref[n], wb_ref[n, :, c0:c0 + MERGE_CHUNK], preferred_element_type=F32)
                    for n in range(N_BRANCH)]
        for off in range(0, MERGE_CHUNK, LANES):
            g = (c0 + off) // LANES
            acc = jax.nn.sigmoid(gates[0][g].astype(F32)) * branches[0][:, off:off + LANES]
            for n in range(1, N_BRANCH):
                acc = acc + jax.nn.sigmoid(gates[n][g].astype(F32)) * branches[n][:, off:off + LANES]
            out_ref[:, c0 + off:c0 + off + LANES] = acc.astype(BF16)


MERGE_CHUNK = 256


def _merge_call(o_a, o_b, o_c, proj, wb, blk):
    n = proj.shape[1]
    d = wb.shape[2]
    tm = 512
    gpb = d // LANES
    heads = lambda c: pl.BlockSpec((N_HEADS, tm, LANES), lambda i: (c, i, 0))
    gate = lambda b: pl.BlockSpec((gpb, tm, LANES), lambda i: (b, i, 0))
    return pl.pallas_call(
        _merge_kernel,
        grid=(n // tm,),
        in_specs=[heads(0), heads(0), heads(0), heads(blk["g_a"]), heads(blk["g_b"]), heads(blk["g_c"]),
                  gate(0), gate(1), gate(2),
                  pl.BlockSpec((N_BRANCH, BRANCH_W, d), lambda i: (0, 0, 0), pipeline_mode=pl.Buffered(1))],
        out_specs=pl.BlockSpec((tm, d), lambda i: (i, 0)),
        out_shape=jax.ShapeDtypeStruct((n, d), BF16),
        scratch_shapes=[pltpu.VMEM((N_BRANCH, tm, BRANCH_W), BF16)],
        compiler_params=_params("arbitrary"),
        name="gate_branch_merge",
    )(o_a, o_b, o_c, proj, proj, proj, proj, proj, proj, wb)


def _out_kernel(m_ref, w_ref, x_ref, gate_ref, g_ref, b_ref, o_ref, *, alpha):
    out = jnp.dot(m_ref[...], w_ref[...], preferred_element_type=F32)
    r = alpha * x_ref[0] + gate_ref[0] * out
    mu = jnp.mean(r, axis=-1, keepdims=True)
    rc = r - mu
    var = jnp.mean(rc * rc, axis=-1, keepdims=True)
    o_ref[0] = rc * lax.rsqrt(var + LN_EPS) * g_ref[...] + b_ref[...]


def _out_call(merged, w_out, x, gate, ln_g, ln_b, alpha):
    b, s, d = x.shape
    tm = 512
    per_seq = s // tm
    return pl.pallas_call(
        functools.partial(_out_kernel, alpha=alpha),
        grid=(b, per_seq),
        in_specs=[pl.BlockSpec((tm, d), lambda i, j: (i * per_seq + j, 0)),
                  pl.BlockSpec((d, d), lambda i, j: (0, 0)),
                  pl.BlockSpec((1, tm, d), lambda i, j: (i, j, 0)),
                  pl.BlockSpec((1, 1, d), lambda i, j: (i, 0, 0)),
                  pl.BlockSpec((1, d), lambda i, j: (0, 0)),
                  pl.BlockSpec((1, d), lambda i, j: (0, 0))],
        out_specs=pl.BlockSpec((1, tm, d), lambda i, j: (i, j, 0)),
        out_shape=jax.ShapeDtypeStruct((b, s, d), F32),
        compiler_params=_params("arbitrary", "arbitrary"),
        name="out_proj_ln",
    )(merged, w_out, x, gate, ln_g, ln_b)


def _swap_halves(w):
    half = w.shape[-1] // 2
    return jnp.concatenate([w[..., half:], w[..., :half]], axis=-1)


def _in_proj_columns(d):
    sizes = (Q_LORA, KV_LORA, MLA_ROPE, BRANCH_W, 3 * BRANCH_W, N_HEADS, BRANCH_W, 3 * BRANCH_W, BRANCH_W,
             N_BRANCH * d)
    starts = [sum(sizes[:i]) for i in range(len(sizes) + 1)]
    k_rope, g_a, f_logit, g_b, merge, end = starts[2], starts[3], starts[5], starts[6], starts[9], starts[10]
    runs = ((merge, end), (g_a, f_logit), (g_b, merge))
    return k_rope, f_logit, runs


def _relayout_kernel(w_ref, scale_ref, main_ref, lat_ref, *, k_rope, f_logit, runs):
    rows = w_ref.shape[1]
    chunk = 8 * LANES
    out = 0
    for lo, hi in runs:
        for c in range(lo, hi, chunk):
            n = min(chunk, hi - c)
            main_ref[0, :, out:out + n] = (w_ref[0, :, c:c + n] * scale_ref[:, out:out + n]).astype(BF16)
            out += n
    lat_ref[0, :, 0:k_rope] = w_ref[0, :, 0:k_rope].astype(BF16)
    lane = lax.broadcasted_iota(jnp.int32, (rows, LANES), 1)
    half = MLA_ROPE // 2
    r = w_ref[0, :, k_rope:k_rope + LANES]
    r = jnp.where(lane < MLA_ROPE, r,
                  jnp.where(lane < MLA_ROPE + half, pltpu.roll(r, half, 1), pltpu.roll(r, LANES - half, 1)))
    lat_ref[0, :, k_rope:k_rope + LANES] = r.astype(BF16)
    f_base = f_logit // LANES * LANES
    f = pltpu.roll(w_ref[0, :, f_base:f_base + LANES], LANES - (f_logit - f_base), 1)
    lat_ref[0, :, k_rope + LANES:k_rope + 2 * LANES] = jnp.where(lane < N_HEADS, f, 0.0).astype(BF16)


def _relayout_call(w_in):
    depth, d, ncol = w_in.shape
    k_rope, f_logit, runs = _in_proj_columns(d)
    assert k_rope % LANES == 0 and k_rope + 2 * LANES == LAT_W
    n_main = sum(hi - lo for lo, hi in runs)
    s = LOG2E / math.sqrt(HEAD_DIM)
    scale = jnp.ones((1, n_main), F32)
    q_b = N_BRANCH * d + BRANCH_W
    q_c = q_b + 3 * BRANCH_W + BRANCH_W
    scale = scale.at[:, q_b:q_b + BRANCH_W].set(s).at[:, q_c:q_c + BRANCH_W].set(s)
    tr = 64
    return pl.pallas_call(
        functools.partial(_relayout_kernel, k_rope=k_rope, f_logit=f_logit, runs=runs),
        grid=(depth, d // tr),
        in_specs=[pl.BlockSpec((1, tr, ncol), lambda l, i: (l, i, 0)),
                  pl.BlockSpec((1, n_main), lambda l, i: (0, 0))],
        out_specs=[pl.BlockSpec((1, tr, n_main), lambda l, i: (l, i, 0)),
                   pl.BlockSpec((1, tr, LAT_W), lambda l, i: (l, i, 0))],
        out_shape=[jax.ShapeDtypeStruct((depth, d, n_main), BF16),
                   jax.ShapeDtypeStruct((depth, d, LAT_W), BF16)],
        compiler_params=_params("arbitrary", "arbitrary"),
        name="w_in_relayout",
    )(w_in, scale)


def _prepare_weights(w_in, w_uq, w_ukv, w_branch, w_out):
    depth = w_in.shape[0]
    w_main, w_lat = _relayout_call(w_in)
    uq = w_uq.reshape(depth, Q_LORA, N_HEADS, MLA_NOPE + MLA_ROPE)
    rope_cols = uq[..., MLA_NOPE:]
    uq = jnp.concatenate([uq[..., :MLA_NOPE], rope_cols, _swap_halves(rope_cols)], axis=-1)
    uq = uq.reshape(depth, Q_LORA, N_HEADS * MLA_QK).astype(BF16)
    return w_lat, w_main, uq, w_ukv.astype(BF16), w_branch.astype(BF16), w_out.astype(BF16)


def _proj_blocks(d):
    first = N_BRANCH * d // (N_HEADS * LANES)
    names = ("g_a", "q_b", "k_b", "v_b", "g_b", "q_c", "k_c", "v_c", "g_c")
    return {name: first + i for i, name in enumerate(names)}


def kernel(x, c, positions, w_ada, b_ada, w_in, q_norm_g, kv_norm_g, w_uq, w_ukv, fox_bias, w_branch, w_out,
           ln_g, ln_b):
    b, s, d = x.shape
    depth = w_in.shape[0]
    alpha = (2 * depth) ** 0.25
    blk = _proj_blocks(d)

    w_lat, w_main, uq, ukv, wb, wo = _prepare_weights(w_in, w_uq, w_ukv, w_branch, w_out)

    c_pad = jnp.zeros((8, d), BF16).at[:b].set(c.astype(BF16))
    mod = _ada_call(c_pad, w_ada, b_ada)[:, :b].reshape(depth, b, 3, 1, d)

    half = MLA_ROPE // 2
    inv_freq = ROPE_THETA ** (-jnp.arange(half, dtype=F32) / half)
    invf = jnp.concatenate([inv_freq, inv_freq, jnp.zeros((LANES - MLA_ROPE,), F32)]).reshape(1, LANES)
    cc, ss = _rope_call(positions, invf)

    fb = jnp.zeros((depth, 1, LANES), F32).at[:, 0, :N_HEADS].set(fox_bias)

    for l in range(depth):
        shift, scale, gate = mod[l, :, 0], mod[l, :, 1], mod[l, :, 2]
        u, q_a, k_a, v_a, frep, frow = _lat_call(
            x, shift, scale, w_lat[l], q_norm_g[l].reshape(1, -1), kv_norm_g[l].reshape(1, -1),
            uq[l], ukv[l], cc, ss, fb[l])
        proj = _proj_call(u.reshape(b * s, d), w_main[l])
        proj4 = proj.reshape(proj.shape[0], b, s, LANES)

        o_a = _attn_call(functools.partial(_softmax_kernel, forget=False), "mla_attention",
                         q_a, k_a, v_a, 0, 0, 0)
        t = ATT_BLOCK
        o_b = _attn_call(functools.partial(_softmax_kernel, forget=True), "fox_attention",
                         proj4, proj4, proj4, blk["q_b"], blk["k_b"], blk["v_b"],
                         extra=(frep, frow),
                         extra_specs=(pl.BlockSpec((N_HEADS, 1, s, LANES), lambda i, j: (0, i, 0, 0)),
                                      pl.BlockSpec((1, s // t, N_HEADS, t), lambda i, j: (i, 0, 0, 0))))
        o_c = _attn_call(_stick_kernel, "stick_attention",
                         proj4, proj4, proj4, blk["q_c"], blk["k_c"], blk["v_c"], softmax=False)

        merged = _merge_call(o_a.reshape(N_HEADS, b * s, LANES), o_b.reshape(N_HEADS, b * s, LANES),
                             o_c.reshape(N_HEADS, b * s, LANES), proj, wb[l], blk)
        x = _out_call(merged, wo[l], x, gate, ln_g[l].reshape(1, d), ln_b[l].reshape(1, d), alpha)
    return x
```

```python
import functools
import math

import jax
import jax.numpy as jnp
from jax import lax
from jax.experimental import pallas as pl
from jax.experimental.pallas import tpu as pltpu

F32 = jnp.float32
BF16 = jnp.bfloat16

N_HEADS = 8
HEAD_DIM = 128
MLA_NOPE = 128
MLA_ROPE = 64
MLA_QK = 2 * HEAD_DIM
Q_LORA = 512
KV_LORA = 256
CHUNK = 64
N_BRANCH = 3
BRANCH_W = N_HEADS * HEAD_DIM
ROPE_THETA = 10000.0
LN_EPS = 1e-5
RMS_EPS = 1e-6
NEG_INF = -1e30
LOG2E = math.log2(math.e)
LANES = 128
LAT_W = 1024
VMEM_LIMIT = 52 * 1024 * 1024

ATT_BLOCK = 256
LAT_ROWS = 256


def _params(*sem):
    return pltpu.CompilerParams(dimension_semantics=sem, vmem_limit_bytes=VMEM_LIMIT)


def _log_sigmoid(x):
    return jnp.minimum(x, 0.0) - jnp.log1p(jnp.exp(-jnp.abs(x)))


def _ada_kernel(c_ref, w_ref, b_ref, o_ref):
    w = w_ref[0].astype(BF16)
    o_ref[0] = jnp.dot(c_ref[...], w, preferred_element_type=F32) + b_ref[0]


def _ada_call(c_pad, w_ada, b_ada):
    depth, d, n3 = w_ada.shape
    tn = 1024
    return pl.pallas_call(
        _ada_kernel,
        grid=(depth, n3 // tn),
        in_specs=[pl.BlockSpec((8, d), lambda l, j: (0, 0)),
                  pl.BlockSpec((1, d, tn), lambda l, j: (l, 0, j)),
                  pl.BlockSpec((1, 1, tn), lambda l, j: (l, 0, j))],
        out_specs=pl.BlockSpec((1, 8, tn), lambda l, j: (l, 0, j)),
        out_shape=jax.ShapeDtypeStruct((depth, 8, n3), F32),
        compiler_params=_params("arbitrary", "arbitrary"),
        name="ada_mod",
    )(c_pad, w_ada, b_ada.reshape(depth, 1, n3))


def _rope_kernel(pos_ref, invf_ref, cc_ref, ss_ref):
    ang = pos_ref[0].astype(F32) * invf_ref[...]
    lane = lax.broadcasted_iota(jnp.int32, ang.shape, 1)
    cos = jnp.cos(ang)
    sin = jnp.sin(ang)
    half = MLA_ROPE // 2
    cc_ref[0] = jnp.where(lane < MLA_ROPE, cos, 0.0)
    ss_ref[0] = jnp.where(lane < half, -sin, jnp.where(lane < MLA_ROPE, sin, 0.0))


def _rope_call(positions, invf):
    b, s = positions.shape
    ts = 512
    spec = pl.BlockSpec((1, ts, LANES), lambda i, j: (i, j, 0))
    return pl.pallas_call(
        _rope_kernel,
        grid=(b, s // ts),
        in_specs=[pl.BlockSpec((1, ts, 1), lambda i, j: (i, j, 0)),
                  pl.BlockSpec((1, LANES), lambda i, j: (0, 0))],
        out_specs=[spec, spec],
        out_shape=[jax.ShapeDtypeStruct((b, s, LANES), F32)] * 2,
        compiler_params=_params("arbitrary", "arbitrary"),
        name="rope_tables",
    )(positions.reshape(b, s, 1), invf)


def _rope_apply(r, cc, ss):
    return r * cc + pltpu.roll(r, MLA_ROPE, 1) * ss


def _lat_kernel(x_ref, shift_ref, scale_ref, wlat_ref, qg_ref, kvg_ref, wuq_ref, wukv_ref,
                cc_ref, ss_ref, fb_ref,
                u_ref, q_ref, k_ref, v_ref, frep_ref, frow_ref, carry_ref):
    ts = x_ref.shape[1]
    x = x_ref[0]
    mu = jnp.mean(x, axis=-1, keepdims=True)
    xc = x - mu
    var = jnp.mean(xc * xc, axis=-1, keepdims=True)
    y = xc * lax.rsqrt(var + LN_EPS)
    u = (y * (1.0 + scale_ref[0]) + shift_ref[0]).astype(BF16)
    u_ref[0] = u

    lat = _nt(u, wlat_ref[0])
    cc = cc_ref[0]
    ss = ss_ref[0]
    q_scale = LOG2E / math.sqrt(MLA_NOPE + MLA_ROPE)

    cq = lat[:, :Q_LORA]
    cqn = cq * lax.rsqrt(jnp.mean(cq * cq, axis=-1, keepdims=True) + RMS_EPS) * qg_ref[...]
    q = jnp.dot(cqn.astype(BF16), wuq_ref[0], preferred_element_type=F32)
    for h in range(N_HEADS):
        base = h * MLA_QK
        q_ref[h, 0, :, 0:HEAD_DIM] = (q[:, base:base + HEAD_DIM] * q_scale).astype(BF16)
        r = q[:, base + HEAD_DIM:base + MLA_QK]
        q_ref[h, 0, :, HEAD_DIM:MLA_QK] = (_rope_apply(r, cc, ss) * q_scale).astype(BF16)

    ckv = lat[:, Q_LORA:Q_LORA + KV_LORA]
    ckvn = ckv * lax.rsqrt(jnp.mean(ckv * ckv, axis=-1, keepdims=True) + RMS_EPS) * kvg_ref[...]
    kv = jnp.dot(ckvn.astype(BF16), wukv_ref[0], preferred_element_type=F32)
    r_off = Q_LORA + KV_LORA
    kpe = _rope_apply(lat[:, r_off:r_off + LANES], cc, ss).astype(BF16)
    for h in range(N_HEADS):
        base = h * 2 * HEAD_DIM
        k_ref[h, 0, :, 0:HEAD_DIM] = kv[:, base:base + HEAD_DIM].astype(BF16)
        k_ref[h, 0, :, HEAD_DIM:MLA_QK] = kpe
        v_ref[h, 0] = kv[:, base + HEAD_DIM:base + 2 * HEAD_DIM].astype(BF16)

    f_off = r_off + LANES
    logf = _log_sigmoid(lat[:, f_off:f_off + LANES] + fb_ref[...])
    lane = lax.broadcasted_iota(jnp.int32, logf.shape, 1)
    logf = jnp.where(lane < N_HEADS, logf, 0.0)
    row = lax.broadcasted_iota(jnp.int32, (ts, ts), 0)
    col = lax.broadcasted_iota(jnp.int32, (ts, ts), 1)
    tri = jnp.where(row >= col, 1.0, 0.0).astype(BF16)
    p1 = logf.astype(BF16)
    r1 = logf - p1.astype(F32)
    p2 = r1.astype(BF16)
    p3 = (r1 - p2.astype(F32)).astype(BF16)
    cs = (jnp.dot(tri, p1, preferred_element_type=F32)
          + jnp.dot(tri, p2, preferred_element_type=F32)
          + jnp.dot(tri, p3, preferred_element_type=F32))

    @pl.when(pl.program_id(1) == 0)
    def _():
        carry_ref[...] = jnp.zeros_like(carry_ref)

    fc = cs + carry_ref[...]
    carry_ref[...] = fc[ts - 1:ts, :]
    fc2 = fc * LOG2E
    frow_ref[0, 0] = fc2.T[:N_HEADS, :]
    for h in range(N_HEADS):
        frep_ref[h, 0] = jnp.broadcast_to(fc2[:, h:h + 1], (ts, LANES))


def _lat_call(x, shift, scale, wlat, qg, kvg, wuq, wukv, cc, ss, fb, layer):
    b, s, d = x.shape
    ts = LAT_ROWS
    const = lambda shape: pl.BlockSpec(shape, lambda i, j: (0,) * len(shape))
    stacked = lambda shape: pl.BlockSpec((1,) + shape, lambda i, j: (layer,) + (0,) * len(shape))
    tok = lambda w: pl.BlockSpec((1, ts, w), lambda i, j: (i, j, 0))
    per_batch = pl.BlockSpec((1, 1, d), lambda i, j: (i, 0, 0))
    head = lambda w: pl.BlockSpec((N_HEADS, 1, ts, w), lambda i, j: (0, i, j, 0))
    return pl.pallas_call(
        _lat_kernel,
        grid=(b, s // ts),
        in_specs=[tok(d), per_batch, per_batch, stacked((LAT_W, d)), const((1, Q_LORA)), const((1, KV_LORA)),
                  stacked((Q_LORA, N_HEADS * MLA_QK)), stacked((KV_LORA, N_HEADS * 2 * HEAD_DIM)),
                  tok(LANES), tok(LANES), const((1, LANES))],
        out_specs=[tok(d), head(MLA_QK), head(MLA_QK), head(HEAD_DIM), head(LANES),
                   pl.BlockSpec((1, 1, N_HEADS, ts), lambda i, j: (i, j, 0, 0))],
        out_shape=[jax.ShapeDtypeStruct((b, s, d), BF16),
                   jax.ShapeDtypeStruct((N_HEADS, b, s, MLA_QK), BF16),
                   jax.ShapeDtypeStruct((N_HEADS, b, s, MLA_QK), BF16),
                   jax.ShapeDtypeStruct((N_HEADS, b, s, HEAD_DIM), BF16),
                   jax.ShapeDtypeStruct((N_HEADS, b, s, LANES), F32),
                   jax.ShapeDtypeStruct((b, s // ts, N_HEADS, ts), F32)],
        scratch_shapes=[pltpu.VMEM((1, LANES), F32)],
        compiler_params=_params("arbitrary", "arbitrary"),
        name="ln_latent",
    )(x, shift, scale, wlat, qg, kvg, wuq, wukv, cc, ss, fb)


def _nt(a, b):
    return lax.dot_general(a, b, (((1,), (1,)), ((), ())), preferred_element_type=F32)


def _proj_kernel(u_ref, w_ref, o_ref):
    acc = _nt(u_ref[...], w_ref[0])
    for g in range(o_ref.shape[0]):
        o_ref[g] = acc[:, g * LANES:(g + 1) * LANES].astype(BF16)


def _proj_call(u, w, layer):
    n, d = u.shape
    ncol = w.shape[1]
    tm, tn = 1024, 1024
    return pl.pallas_call(
        _proj_kernel,
        grid=(n // tm, ncol // tn),
        in_specs=[pl.BlockSpec((tm, d), lambda i, j: (i, 0)),
                  pl.BlockSpec((1, tn, d), lambda i, j: (layer, j, 0))],
        out_specs=pl.BlockSpec((tn // LANES, tm, LANES), lambda i, j: (j, i, 0)),
        out_shape=jax.ShapeDtypeStruct((ncol // LANES, n, LANES), BF16),
        compiler_params=_params("arbitrary", "arbitrary"),
        name="in_proj",
    )(u, w)


def _block_iotas(t):
    row = lax.broadcasted_iota(jnp.int32, (t, t), 0)
    col = lax.broadcasted_iota(jnp.int32, (t, t), 1)
    return row, col


def _kq(k, q):
    return lax.dot_general(k, q, (((1,), (1,)), ((), ())), preferred_element_type=F32)


def _stage_values(v_ref, vt_ref, ones_rows):
    n_blk, t = vt_ref.shape[1], vt_ref.shape[3]

    @pl.when(pl.program_id(1) == 0)
    def _():
        def per_head(h, carry):
            for j in range(n_blk):
                vt = v_ref[h, 0, j * t:(j + 1) * t, :].astype(F32).T
                vt_ref[h, j, 0:HEAD_DIM, :] = vt.astype(BF16)
                if ones_rows:
                    vt_ref[h, j, HEAD_DIM:HEAD_DIM + ones_rows, :] = jnp.ones((ones_rows, t), BF16)
            return carry
        lax.fori_loop(0, N_HEADS, per_head, 0)


SUM_ROWS = 8
DEAD_LOG2 = -150.0


def _softmax_kernel(*refs, forget):
    if forget:
        q_ref, k_ref, v_ref, frep_ref, frow_ref, o_ref, vt_ref, m_ref, acc_ref, s_ref = refs
    else:
        q_ref, k_ref, v_ref, o_ref, vt_ref, m_ref, acc_ref, s_ref = refs
    t = q_ref.shape[2]
    qi = pl.program_id(1)
    key, qry = _block_iotas(t)
    mask = (key <= qry) if forget else ((key // CHUNK) <= (qry // CHUNK))

    _stage_values(v_ref, vt_ref, SUM_ROWS)
    m_ref[...] = jnp.full(m_ref.shape, NEG_INF, F32)
    acc_ref[...] = jnp.zeros(acc_ref.shape, F32)

    def scores(j, h):
        off = pl.multiple_of(j * t, t)
        s_ref[h] = _kq(k_ref[h, 0, pl.ds(off, t), :], q_ref[h, 0])

    def consume(j, h, diagonal):
        s = s_ref[h]
        if forget:
            off = pl.multiple_of(j * t, t)
            fk = frep_ref[h, 0, pl.ds(off, t), :]
            s = s + (frow_ref[0, qi, h:h + 1, :] - jnp.concatenate([fk] * (t // LANES), axis=1))
        if diagonal:
            s = jnp.where(mask, s, NEG_INF)
        m_old = m_ref[h]
        m_new = jnp.maximum(m_old, jnp.max(s, axis=0, keepdims=True))
        p = jnp.exp2(s - m_new).astype(BF16)
        acc_ref[h] = (jnp.exp2(m_old - m_new) * acc_ref[h]
                      + jnp.dot(vt_ref[h, j], p, preferred_element_type=F32))
        m_ref[h] = m_new

    for h in range(N_HEADS):
        scores(0, h)

    def body(j, carry):
        for h in range(N_HEADS):
            consume(j, h, False)
            scores(j + 1, h)
        return carry

    lax.fori_loop(0, qi, body, 0)
    for h in range(N_HEADS):
        consume(qi, h, True)
        acc = acc_ref[h]
        o_ref[h, 0] = (acc[:HEAD_DIM] / acc[HEAD_DIM:HEAD_DIM + 1]).T.astype(BF16)


def _stick_kernel(q_ref, k_ref, v_ref, o_ref, vt_ref, c_ref, acc_ref, s_ref, lb_ref, loc_ref, first_ref):
    t = q_ref.shape[2]
    qi = pl.program_id(1)
    key, qry = _block_iotas(t)
    strict = key < qry
    later = jnp.where(qry > key, 1.0, 0.0).astype(BF16)

    _stage_values(v_ref, vt_ref, 0)
    c_ref[...] = jnp.zeros(c_ref.shape, F32)
    acc_ref[...] = jnp.zeros(acc_ref.shape, F32)

    def scores(j, h):
        off = pl.multiple_of(j * t, t)
        s_ref[h] = _kq(k_ref[h, 0, pl.ds(off, t), :], q_ref[h, 0])

    def decay(h, diagonal):
        z = s_ref[h]
        log_beta = jnp.minimum(z, 0.0) - jnp.log2(1.0 + jnp.exp2(-jnp.abs(z)))
        log_1mb = log_beta - z
        if diagonal:
            log_1mb = jnp.where(strict, log_1mb, 0.0)
        hi = log_1mb.astype(BF16)
        lo = (log_1mb - hi.astype(F32)).astype(BF16)
        lb_ref[h] = log_beta
        first_ref[h] = log_1mb[0:1, :]
        loc_ref[h] = (jnp.dot(later, hi, preferred_element_type=F32)
                      + jnp.dot(later, lo, preferred_element_type=F32))

    def weigh(j, h, diagonal):
        local = loc_ref[h]
        c = c_ref[h]
        a = jnp.exp2(lb_ref[h] + (local + c))
        if diagonal:
            a = jnp.where(strict, a, 0.0)
        acc_ref[h] = acc_ref[h] + jnp.dot(vt_ref[h, j], a.astype(BF16), preferred_element_type=F32)
        c_ref[h] = c + (local[0:1, :] + first_ref[h])

    def step(j, diagonal, prefetch):
        for h in range(N_HEADS):
            decay(h, diagonal)
            if prefetch is not None:
                scores(prefetch, h)
        for h in range(N_HEADS):
            weigh(j, h, diagonal)

    for h in range(N_HEADS):
        scores(qi, h)

    @pl.when(qi == 0)
    def _():
        step(0, True, None)

    def any_weight_left():
        c_max = jnp.max(jnp.max(c_ref[...], axis=0), axis=1, keepdims=True)
        return c_max[0, 0] > DEAD_LOG2

    @pl.when(qi > 0)
    def _():
        step(qi, True, qi - 1)

        def body(state):
            j, _ = state
            step(j, False, jnp.maximum(j - 1, 0))
            return j - 1, any_weight_left()

        lax.while_loop(lambda state: jnp.logical_and(state[0] >= 0, state[1]), body,
                       (qi - 1, any_weight_left()))

    for h in range(N_HEADS):
        o_ref[h, 0] = acc_ref[h].T.astype(BF16)


def _attn_call(kernel, name, q, k, v, q_blk, k_blk, v_blk, extra=(), extra_specs=(), softmax=True):
    _, b, s, _ = q.shape
    t = ATT_BLOCK
    qw, kw, vw = q.shape[3], k.shape[3], v.shape[3]
    rows = HEAD_DIM + (SUM_ROWS if softmax else 0)
    vt = pltpu.VMEM((N_HEADS, s // t, rows, t), BF16)
    stat = pltpu.VMEM((N_HEADS, 1, t), F32)
    acc = pltpu.VMEM((N_HEADS, rows, t), F32)
    tile = pltpu.VMEM((N_HEADS, t, t), F32)
    return pl.pallas_call(
        kernel,
        grid=(b, s // t),
        in_specs=[pl.BlockSpec((N_HEADS, 1, t, qw), lambda i, j: (q_blk, i, j, 0)),
                  pl.BlockSpec((N_HEADS, 1, s, kw), lambda i, j: (k_blk, i, 0, 0)),
                  pl.BlockSpec((N_HEADS, 1, s, vw), lambda i, j: (v_blk, i, 0, 0)),
                  *extra_specs],
        out_specs=pl.BlockSpec((N_HEADS, 1, t, HEAD_DIM), lambda i, j: (0, i, j, 0)),
        out_shape=jax.ShapeDtypeStruct((N_HEADS, b, s, HEAD_DIM), BF16),
        scratch_shapes=[vt, stat, acc, tile] if softmax else [vt, stat, acc, tile, tile, tile, stat],
        compiler_params=_params("arbitrary", "arbitrary"),
        name=name,
    )(q, k, v, *extra)


def _merge_kernel(oa_ref, ob_ref, oc_ref, ga_ref, gb_ref, gc_ref, ma_ref, mb_ref, mc_ref, wb_ref,
                  out_ref, ys_ref):
    for n, (o_ref, g_ref) in enumerate(((oa_ref, ga_ref), (ob_ref, gb_ref), (oc_ref, gc_ref))):
        for h in range(N_HEADS):
            g = g_ref[h].astype(F32)
            ys_ref[n, :, h * HEAD_DIM:(h + 1) * HEAD_DIM] = (
                o_ref[h].astype(F32) * (g * jax.nn.sigmoid(g))).astype(BF16)

    gates = (ma_ref, mb_ref, mc_ref)
    for c0 in range(0, out_ref.shape[1], MERGE_CHUNK):
        branches = [jnp.dot(ys_ref[n], wb_ref[0, n, :, c0:c0 + MERGE_CHUNK], preferred_element_type=F32)
                    for n in range(N_BRANCH)]
        for off in range(0, MERGE_CHUNK, LANES):
            g = (c0 + off) // LANES
            acc = jax.nn.sigmoid(gates[0][g].astype(F32)) * branches[0][:, off:off + LANES]
            for n in range(1, N_BRANCH):
                acc = acc + jax.nn.sigmoid(gates[n][g].astype(F32)) * branches[n][:, off:off + LANES]
            out_ref[:, c0 + off:c0 + off + LANES] = acc.astype(BF16)


MERGE_CHUNK = 256


def _merge_call(o_a, o_b, o_c, proj, wb, blk, layer):
    n = proj.shape[1]
    d = wb.shape[3]
    tm = 512
    gpb = d // LANES
    heads = lambda c: pl.BlockSpec((N_HEADS, tm, LANES), lambda i: (c, i, 0))
    gate = lambda b: pl.BlockSpec((gpb, tm, LANES), lambda i: (b, i, 0))
    return pl.pallas_call(
        _merge_kernel,
        grid=(n // tm,),
        in_specs=[heads(0), heads(0), heads(0), heads(blk["g_a"]), heads(blk["g_b"]), heads(blk["g_c"]),
                  gate(0), gate(1), gate(2),
                  pl.BlockSpec((1, N_BRANCH, BRANCH_W, d), lambda i: (layer, 0, 0, 0),
                               pipeline_mode=pl.Buffered(1))],
        out_specs=pl.BlockSpec((tm, d), lambda i: (i, 0)),
        out_shape=jax.ShapeDtypeStruct((n, d), BF16),
        scratch_shapes=[pltpu.VMEM((N_BRANCH, tm, BRANCH_W), BF16)],
        compiler_params=_params("arbitrary"),
        name="gate_branch_merge",
    )(o_a, o_b, o_c, proj, proj, proj, proj, proj, proj, wb)


def _out_kernel(m_ref, w_ref, x_ref, gate_ref, g_ref, b_ref, o_ref, *, alpha):
    out = jnp.dot(m_ref[...], w_ref[0], preferred_element_type=F32)
    r = alpha * x_ref[0] + gate_ref[0] * out
    mu = jnp.mean(r, axis=-1, keepdims=True)
    rc = r - mu
    var = jnp.mean(rc * rc, axis=-1, keepdims=True)
    o_ref[0] = rc * lax.rsqrt(var + LN_EPS) * g_ref[...] + b_ref[...]


def _out_call(merged, w_out, x, gate, ln_g, ln_b, alpha, layer):
    b, s, d = x.shape
    tm = 512
    per_seq = s // tm
    return pl.pallas_call(
        functools.partial(_out_kernel, alpha=alpha),
        grid=(b, per_seq),
        in_specs=[pl.BlockSpec((tm, d), lambda i, j: (i * per_seq + j, 0)),
                  pl.BlockSpec((1, d, d), lambda i, j: (layer, 0, 0)),
                  pl.BlockSpec((1, tm, d), lambda i, j: (i, j, 0)),
                  pl.BlockSpec((1, 1, d), lambda i, j: (i, 0, 0)),
                  pl.BlockSpec((1, d), lambda i, j: (0, 0)),
                  pl.BlockSpec((1, d), lambda i, j: (0, 0))],
        out_specs=pl.BlockSpec((1, tm, d), lambda i, j: (i, j, 0)),
        out_shape=jax.ShapeDtypeStruct((b, s, d), F32),
        compiler_params=_params("arbitrary", "arbitrary"),
        name="out_proj_ln",
    )(merged, w_out, x, gate, ln_g, ln_b)


def _swap_halves(w):
    half = w.shape[-1] // 2
    return jnp.concatenate([w[..., half:], w[..., :half]], axis=-1)


def _in_proj_columns(d):
    sizes = (Q_LORA, KV_LORA, MLA_ROPE, BRANCH_W, 3 * BRANCH_W, N_HEADS, BRANCH_W, 3 * BRANCH_W, BRANCH_W,
             N_BRANCH * d)
    starts = [sum(sizes[:i]) for i in range(len(sizes) + 1)]
    k_rope, g_a, f_logit, g_b, merge, end = starts[2], starts[3], starts[5], starts[6], starts[9], starts[10]
    runs = ((merge, end), (g_a, f_logit), (g_b, merge))
    return k_rope, f_logit, runs


def _relayout_kernel(w_ref, main_ref, lat_ref, *, k_rope, f_logit, runs, scaled_rows):
    chunk = 8 * LANES
    s = LOG2E / math.sqrt(HEAD_DIM)
    out = 0
    for lo, hi in runs:
        for c in range(lo, hi, chunk):
            n = min(chunk, hi - c)
            w = w_ref[0, c:c + n, :]
            if out in scaled_rows:
                w = w * s
            main_ref[0, out:out + n, :] = w.astype(BF16)
            out += n
    half = MLA_ROPE // 2
    swapped = k_rope + MLA_ROPE
    lat_ref[0, 0:swapped, :] = w_ref[0, 0:swapped, :].astype(BF16)
    lat_ref[0, swapped:swapped + half, :] = w_ref[0, k_rope + half:swapped, :].astype(BF16)
    lat_ref[0, swapped + half:swapped + MLA_ROPE, :] = w_ref[0, k_rope:k_rope + half, :].astype(BF16)
    pad = jnp.zeros((LANES - N_HEADS, w_ref.shape[2]), F32)
    forget = jnp.concatenate([w_ref[0, f_logit:f_logit + N_HEADS, :], pad], axis=0)
    lat_ref[0, swapped + MLA_ROPE:, :] = forget.astype(BF16)


def _relayout_call(w_in):
    depth, d, n_in = w_in.shape
    k_rope, f_logit, runs = _in_proj_columns(d)
    assert k_rope + 2 * MLA_ROPE + LANES == LAT_W
    n_main = sum(hi - lo for lo, hi in runs)
    q_b = N_BRANCH * d + BRANCH_W
    q_c = q_b + 3 * BRANCH_W + BRANCH_W
    tc = LANES
    return pl.pallas_call(
        functools.partial(_relayout_kernel, k_rope=k_rope, f_logit=f_logit, runs=runs, scaled_rows=(q_b, q_c)),
        grid=(depth, d // tc),
        in_specs=[pl.BlockSpec((1, n_in, tc), lambda l, i: (l, 0, i))],
        out_specs=[pl.BlockSpec((1, n_main, tc), lambda l, i: (l, 0, i)),
                   pl.BlockSpec((1, LAT_W, tc), lambda l, i: (l, 0, i))],
        out_shape=[jax.ShapeDtypeStruct((depth, n_main, d), BF16),
                   jax.ShapeDtypeStruct((depth, LAT_W, d), BF16)],
        compiler_params=_params("arbitrary", "arbitrary"),
        name="w_in_relayout",
    )(jnp.swapaxes(w_in, 1, 2))


def _prepare_weights(w_in, w_uq, w_ukv, w_branch, w_out):
    depth = w_in.shape[0]
    w_main, w_lat = _relayout_call(w_in)
    uq = w_uq.reshape(depth, Q_LORA, N_HEADS, MLA_NOPE + MLA_ROPE)
    rope_cols = uq[..., MLA_NOPE:]
    uq = jnp.concatenate([uq[..., :MLA_NOPE], rope_cols, _swap_halves(rope_cols)], axis=-1)
    uq = uq.reshape(depth, Q_LORA, N_HEADS * MLA_QK).astype(BF16)
    return w_lat, w_main, uq, w_ukv.astype(BF16), w_branch.astype(BF16), w_out.astype(BF16)


def _proj_blocks(d):
    first = N_BRANCH * d // (N_HEADS * LANES)
    names = ("g_a", "q_b", "k_b", "v_b", "g_b", "q_c", "k_c", "v_c", "g_c")
    return {name: first + i for i, name in enumerate(names)}


def kernel(x, c, positions, w_ada, b_ada, w_in, q_norm_g, kv_norm_g, w_uq, w_ukv, fox_bias, w_branch, w_out,
           ln_g, ln_b):
    b, s, d = x.shape
    depth = w_in.shape[0]
    alpha = (2 * depth) ** 0.25
    blk = _proj_blocks(d)

    w_lat, w_main, uq, ukv, wb, wo = _prepare_weights(w_in, w_uq, w_ukv, w_branch, w_out)

    c_pad = jnp.zeros((8, d), BF16).at[:b].set(c.astype(BF16))
    mod = _ada_call(c_pad, w_ada, b_ada)[:, :b].reshape(depth, b, 3, 1, d)

    half = MLA_ROPE // 2
    inv_freq = ROPE_THETA ** (-jnp.arange(half, dtype=F32) / half)
    invf = jnp.concatenate([inv_freq, inv_freq, jnp.zeros((LANES - MLA_ROPE,), F32)]).reshape(1, LANES)
    cc, ss = _rope_call(positions, invf)

    fb = jnp.zeros((depth, 1, LANES), F32).at[:, 0, :N_HEADS].set(fox_bias)

    for l in range(depth):
        shift, scale, gate = mod[l, :, 0], mod[l, :, 1], mod[l, :, 2]
        u, q_a, k_a, v_a, frep, frow = _lat_call(
            x, shift, scale, w_lat, q_norm_g[l].reshape(1, -1), kv_norm_g[l].reshape(1, -1),
            uq, ukv, cc, ss, fb[l], l)
        proj = _proj_call(u.reshape(b * s, d), w_main, l)
        proj4 = proj.reshape(proj.shape[0], b, s, LANES)

        o_a = _attn_call(functools.partial(_softmax_kernel, forget=False), "mla_attention",
                         q_a, k_a, v_a, 0, 0, 0)
        t = ATT_BLOCK
        o_b = _attn_call(functools.partial(_softmax_kernel, forget=True), "fox_attention",
                         proj4, proj4, proj4, blk["q_b"], blk["k_b"], blk["v_b"],
                         extra=(frep, frow),
                         extra_specs=(pl.BlockSpec((N_HEADS, 1, s, LANES), lambda i, j: (0, i, 0, 0)),
                                      pl.BlockSpec((1, s // t, N_HEADS, t), lambda i, j: (i, 0, 0, 0))))
        o_c = _attn_call(_stick_kernel, "stick_attention",
                         proj4, proj4, proj4, blk["q_c"], blk["k_c"], blk["v_c"], softmax=False)

        merged = _merge_call(o_a.reshape(N_HEADS, b * s, LANES), o_b.reshape(N_HEADS, b * s, LANES),
                             o_c.reshape(N_HEADS, b * s, LANES), proj, wb, blk, l)
        x = _out_call(merged, wo, x, gate, ln_g[l].reshape(1, d), ln_b[l].reshape(1, d), alpha, l)
    return x
```

```python
import functools
import math

import jax
import jax.numpy as jnp
from jax import lax
from jax.experimental import pallas as pl
from jax.experimental.pallas import tpu as pltpu

F32 = jnp.float32
BF16 = jnp.bfloat16

N_HEADS = 8
HEAD_DIM = 128
MLA_NOPE = 128
MLA_ROPE = 64
MLA_QK = 2 * HEAD_DIM
Q_LORA = 512
KV_LORA = 256
CHUNK = 64
N_BRANCH = 3
BRANCH_W = N_HEADS * HEAD_DIM
ROPE_THETA = 10000.0
LN_EPS = 1e-5
RMS_EPS = 1e-6
NEG_INF = -1e30
LOG2E = math.log2(math.e)
LANES = 128
SUBLANES = 8
LAT_W = 1024
VMEM_LIMIT = 52 * 1024 * 1024

ATT_BLOCK = 256
LAT_ROWS = 256


def _params(*sem):
    return pltpu.CompilerParams(dimension_semantics=sem, vmem_limit_bytes=VMEM_LIMIT)


def _log_sigmoid(x):
    return jnp.minimum(x, 0.0) - jnp.log1p(jnp.exp(-jnp.abs(x)))


def _ada_kernel(c_ref, w_ref, b_ref, o_ref):
    w = w_ref[0].astype(BF16)
    o_ref[0] = jnp.dot(c_ref[...], w, preferred_element_type=F32) + b_ref[0]


def _ada_call(c_pad, w_ada, b_ada):
    depth, d, n3 = w_ada.shape
    tn = 1024
    return pl.pallas_call(
        _ada_kernel,
        grid=(depth, n3 // tn),
        in_specs=[pl.BlockSpec((8, d), lambda l, j: (0, 0)),
                  pl.BlockSpec((1, d, tn), lambda l, j: (l, 0, j)),
                  pl.BlockSpec((1, 1, tn), lambda l, j: (l, 0, j))],
        out_specs=pl.BlockSpec((1, 8, tn), lambda l, j: (l, 0, j)),
        out_shape=jax.ShapeDtypeStruct((depth, 8, n3), F32),
        compiler_params=_params("arbitrary", "arbitrary"),
        name="ada_mod",
    )(c_pad, w_ada, b_ada.reshape(depth, 1, n3))


def _rope_kernel(pos_ref, invf_ref, cc_ref, ss_ref):
    ang = pos_ref[0].astype(F32) * invf_ref[...]
    lane = lax.broadcasted_iota(jnp.int32, ang.shape, 1)
    cos = jnp.cos(ang)
    sin = jnp.sin(ang)
    half = MLA_ROPE // 2
    cc_ref[0] = jnp.where(lane < MLA_ROPE, cos, 0.0)
    ss_ref[0] = jnp.where(lane < half, -sin, jnp.where(lane < MLA_ROPE, sin, 0.0))


def _rope_call(positions, invf):
    b, s = positions.shape
    ts = 512
    spec = pl.BlockSpec((1, ts, LANES), lambda i, j: (i, j, 0))
    return pl.pallas_call(
        _rope_kernel,
        grid=(b, s // ts),
        in_specs=[pl.BlockSpec((1, ts, 1), lambda i, j: (i, j, 0)),
                  pl.BlockSpec((1, LANES), lambda i, j: (0, 0))],
        out_specs=[spec, spec],
        out_shape=[jax.ShapeDtypeStruct((b, s, LANES), F32)] * 2,
        compiler_params=_params("arbitrary", "arbitrary"),
        name="rope_tables",
    )(positions.reshape(b, s, 1), invf)


def _rope_apply(r, cc, ss):
    return r * cc + pltpu.roll(r, MLA_ROPE, 1) * ss


def _lat_kernel(x_ref, shift_ref, scale_ref, wlat_ref, qg_ref, kvg_ref, wuq_ref, wukv_ref,
                cc_ref, ss_ref, fb_ref,
                u_ref, q_ref, k_ref, v_ref, frep_ref, frow_ref, carry_ref):
    ts = x_ref.shape[1]
    x = x_ref[0]
    mu = jnp.mean(x, axis=-1, keepdims=True)
    xc = x - mu
    var = jnp.mean(xc * xc, axis=-1, keepdims=True)
    y = xc * lax.rsqrt(var + LN_EPS)
    u = (y * (1.0 + scale_ref[0]) + shift_ref[0]).astype(BF16)
    u_ref[0] = u

    lat = _nt(u, wlat_ref[0])
    cc = cc_ref[0]
    ss = ss_ref[0]
    q_scale = LOG2E / math.sqrt(MLA_NOPE + MLA_ROPE)

    cq = lat[:, :Q_LORA]
    cqn = cq * lax.rsqrt(jnp.mean(cq * cq, axis=-1, keepdims=True) + RMS_EPS) * qg_ref[...]
    q = jnp.dot(cqn.astype(BF16), wuq_ref[0], preferred_element_type=F32)
    for h in range(N_HEADS):
        base = h * MLA_QK
        q_ref[h, 0, :, 0:HEAD_DIM] = (q[:, base:base + HEAD_DIM] * q_scale).astype(BF16)
        r = q[:, base + HEAD_DIM:base + MLA_QK]
        q_ref[h, 0, :, HEAD_DIM:MLA_QK] = (_rope_apply(r, cc, ss) * q_scale).astype(BF16)

    ckv = lat[:, Q_LORA:Q_LORA + KV_LORA]
    ckvn = ckv * lax.rsqrt(jnp.mean(ckv * ckv, axis=-1, keepdims=True) + RMS_EPS) * kvg_ref[...]
    kv = jnp.dot(ckvn.astype(BF16), wukv_ref[0], preferred_element_type=F32)
    r_off = Q_LORA + KV_LORA
    kpe = _rope_apply(lat[:, r_off:r_off + LANES], cc, ss).astype(BF16)
    for h in range(N_HEADS):
        base = h * 2 * HEAD_DIM
        k_ref[h, 0, :, 0:HEAD_DIM] = kv[:, base:base + HEAD_DIM].astype(BF16)
        k_ref[h, 0, :, HEAD_DIM:MLA_QK] = kpe
        v_ref[h, 0] = kv[:, base + HEAD_DIM:base + 2 * HEAD_DIM].astype(BF16)

    f_off = r_off + LANES
    logf = _log_sigmoid(lat[:, f_off:f_off + LANES] + fb_ref[...])
    lane = lax.broadcasted_iota(jnp.int32, logf.shape, 1)
    logf = jnp.where(lane < N_HEADS, logf, 0.0)
    row = lax.broadcasted_iota(jnp.int32, (ts, ts), 0)
    col = lax.broadcasted_iota(jnp.int32, (ts, ts), 1)
    tri = jnp.where(row >= col, 1.0, 0.0).astype(BF16)
    p1 = logf.astype(BF16)
    r1 = logf - p1.astype(F32)
    p2 = r1.astype(BF16)
    p3 = (r1 - p2.astype(F32)).astype(BF16)
    cs = (jnp.dot(tri, p1, preferred_element_type=F32)
          + jnp.dot(tri, p2, preferred_element_type=F32)
          + jnp.dot(tri, p3, preferred_element_type=F32))

    @pl.when(pl.program_id(1) == 0)
    def _():
        carry_ref[...] = jnp.zeros_like(carry_ref)

    fc = cs + carry_ref[...]
    carry_ref[...] = fc[ts - 1:ts, :]
    fc2 = fc * LOG2E
    frow_ref[0, 0] = fc2.T[:N_HEADS, :]
    for h in range(N_HEADS):
        frep_ref[h, 0] = jnp.broadcast_to(fc2[:, h:h + 1], (ts, LANES))


def _lat_call(x, shift, scale, wlat, qg, kvg, wuq, wukv, cc, ss, fb, layer):
    b, s, d = x.shape
    ts = LAT_ROWS
    const = lambda shape: pl.BlockSpec(shape, lambda i, j: (0,) * len(shape))
    stacked = lambda shape: pl.BlockSpec((1,) + shape, lambda i, j: (layer,) + (0,) * len(shape))
    tok = lambda w: pl.BlockSpec((1, ts, w), lambda i, j: (i, j, 0))
    per_batch = pl.BlockSpec((1, 1, d), lambda i, j: (i, 0, 0))
    head = lambda w: pl.BlockSpec((N_HEADS, 1, ts, w), lambda i, j: (0, i, j, 0))
    return pl.pallas_call(
        _lat_kernel,
        grid=(b, s // ts),
        in_specs=[tok(d), per_batch, per_batch, stacked((LAT_W, d)), const((1, Q_LORA)), const((1, KV_LORA)),
                  stacked((Q_LORA, N_HEADS * MLA_QK)), stacked((KV_LORA, N_HEADS * 2 * HEAD_DIM)),
                  tok(LANES), tok(LANES), const((1, LANES))],
        out_specs=[tok(d), head(MLA_QK), head(MLA_QK), head(HEAD_DIM), head(LANES),
                   pl.BlockSpec((1, 1, N_HEADS, ts), lambda i, j: (i, j, 0, 0))],
        out_shape=[jax.ShapeDtypeStruct((b, s, d), BF16),
                   jax.ShapeDtypeStruct((N_HEADS, b, s, MLA_QK), BF16),
                   jax.ShapeDtypeStruct((N_HEADS, b, s, MLA_QK), BF16),
                   jax.ShapeDtypeStruct((N_HEADS, b, s, HEAD_DIM), BF16),
                   jax.ShapeDtypeStruct((N_HEADS, b, s, LANES), F32),
                   jax.ShapeDtypeStruct((b, s // ts, N_HEADS, ts), F32)],
        scratch_shapes=[pltpu.VMEM((1, LANES), F32)],
        compiler_params=_params("arbitrary", "arbitrary"),
        name="ln_latent",
    )(x, shift, scale, wlat, qg, kvg, wuq, wukv, cc, ss, fb)


def _nt(a, b):
    return lax.dot_general(a, b, (((1,), (1,)), ((), ())), preferred_element_type=F32)


def _proj_kernel(u_ref, w_ref, o_ref, wb_ref, *, scaled_tiles):
    j = pl.program_id(0)

    @pl.when(pl.program_id(1) == 0)
    def _():
        is_scaled = functools.reduce(jnp.logical_or, [j == t for t in scaled_tiles])
        scale = jnp.where(is_scaled, LOG2E / math.sqrt(HEAD_DIM), 1.0)
        wb_ref[...] = (w_ref[0] * scale).astype(BF16)

    acc = _nt(u_ref[...], wb_ref[...])
    for g in range(o_ref.shape[0]):
        o_ref[g] = acc[:, g * LANES:(g + 1) * LANES].astype(BF16)


def _proj_call(u, wt, layer):
    n, d = u.shape
    _, _, runs = _in_proj_columns(d)
    tm, tn = 1024, 1024
    ncol = sum(hi - lo for lo, hi in runs)
    assert all((hi - lo) % tn == 0 for lo, hi in runs)
    q_b = (N_BRANCH * d + BRANCH_W) // tn
    q_c = q_b + 4 * BRANCH_W // tn

    tile0, firsts = 0, []
    for lo, hi in runs:
        firsts.append((tile0, lo))
        tile0 += (hi - lo) // tn

    def first_row(j):
        t_last, lo_last = firsts[-1]
        row = lo_last + (j - t_last) * tn
        for k in range(len(firsts) - 2, -1, -1):
            t_k, lo_k = firsts[k]
            row = jnp.where(j < firsts[k + 1][0], lo_k + (j - t_k) * tn, row)
        return pl.multiple_of(row, SUBLANES)

    return pl.pallas_call(
        functools.partial(_proj_kernel, scaled_tiles=(q_b, q_c)),
        grid=(ncol // tn, n // tm),
        in_specs=[pl.BlockSpec((tm, d), lambda j, i: (i, 0)),
                  pl.BlockSpec((pl.Element(1), pl.Element(tn), pl.Element(d)),
                               lambda j, i: (layer, first_row(j), 0))],
        out_specs=pl.BlockSpec((tn // LANES, tm, LANES), lambda j, i: (j, i, 0)),
        out_shape=jax.ShapeDtypeStruct((ncol // LANES, n, LANES), BF16),
        scratch_shapes=[pltpu.VMEM((tn, d), BF16)],
        compiler_params=_params("arbitrary", "arbitrary"),
        name="in_proj",
    )(u, wt)


def _block_iotas(t):
    row = lax.broadcasted_iota(jnp.int32, (t, t), 0)
    col = lax.broadcasted_iota(jnp.int32, (t, t), 1)
    return row, col


def _kq(k, q):
    return lax.dot_general(k, q, (((1,), (1,)), ((), ())), preferred_element_type=F32)


def _stage_values(v_ref, vt_ref, ones_rows):
    n_blk, t = vt_ref.shape[1], vt_ref.shape[3]

    @pl.when(pl.program_id(1) == 0)
    def _():
        def per_head(h, carry):
            for j in range(n_blk):
                vt = v_ref[h, 0, j * t:(j + 1) * t, :].astype(F32).T
                vt_ref[h, j, 0:HEAD_DIM, :] = vt.astype(BF16)
                if ones_rows:
                    vt_ref[h, j, HEAD_DIM:HEAD_DIM + ones_rows, :] = jnp.ones((ones_rows, t), BF16)
            return carry
        lax.fori_loop(0, N_HEADS, per_head, 0)


SUM_ROWS = 8
DEAD_LOG2 = -150.0


def _softmax_kernel(*refs, forget):
    if forget:
        q_ref, k_ref, v_ref, frep_ref, frow_ref, o_ref, vt_ref, m_ref, acc_ref, s_ref = refs
    else:
        q_ref, k_ref, v_ref, o_ref, vt_ref, m_ref, acc_ref, s_ref = refs
    t = q_ref.shape[2]
    qi = pl.program_id(1)
    key, qry = _block_iotas(t)
    mask = (key <= qry) if forget else ((key // CHUNK) <= (qry // CHUNK))

    _stage_values(v_ref, vt_ref, SUM_ROWS)
    m_ref[...] = jnp.full(m_ref.shape, NEG_INF, F32)
    acc_ref[...] = jnp.zeros(acc_ref.shape, F32)

    def scores(j, h):
        off = pl.multiple_of(j * t, t)
        s_ref[h] = _kq(k_ref[h, 0, pl.ds(off, t), :], q_ref[h, 0])

    def consume(j, h, diagonal):
        s = s_ref[h]
        if forget:
            off = pl.multiple_of(j * t, t)
            fk = frep_ref[h, 0, pl.ds(off, t), :]
            s = s + (frow_ref[0, qi, h:h + 1, :] - jnp.concatenate([fk] * (t // LANES), axis=1))
        if diagonal:
            s = jnp.where(mask, s, NEG_INF)
        m_old = m_ref[h]
        m_new = jnp.maximum(m_old, jnp.max(s, axis=0, keepdims=True))
        p = jnp.exp2(s - m_new).astype(BF16)
        acc_ref[h] = (jnp.exp2(m_old - m_new) * acc_ref[h]
                      + jnp.dot(vt_ref[h, j], p, preferred_element_type=F32))
        m_ref[h] = m_new

    for h in range(N_HEADS):
        scores(0, h)

    def body(j, carry):
        for h in range(N_HEADS):
            consume(j, h, False)
            scores(j + 1, h)
        return carry

    lax.fori_loop(0, qi, body, 0)
    for h in range(N_HEADS):
        consume(qi, h, True)
        acc = acc_ref[h]
        o_ref[h, 0] = (acc[:HEAD_DIM] / acc[HEAD_DIM:HEAD_DIM + 1]).T.astype(BF16)


def _stick_kernel(q_ref, k_ref, v_ref, o_ref, vt_ref, c_ref, acc_ref, s_ref, lb_ref, loc_ref, first_ref):
    t = q_ref.shape[2]
    qi = pl.program_id(1)
    key, qry = _block_iotas(t)
    strict = key < qry
    later = jnp.where(qry > key, 1.0, 0.0).astype(BF16)

    _stage_values(v_ref, vt_ref, 0)
    c_ref[...] = jnp.zeros(c_ref.shape, F32)
    acc_ref[...] = jnp.zeros(acc_ref.shape, F32)

    def scores(j, h):
        off = pl.multiple_of(j * t, t)
        s_ref[h] = _kq(k_ref[h, 0, pl.ds(off, t), :], q_ref[h, 0])

    def decay(h, diagonal):
        z = s_ref[h]
        log_beta = jnp.minimum(z, 0.0) - jnp.log2(1.0 + jnp.exp2(-jnp.abs(z)))
        log_1mb = log_beta - z
        if diagonal:
            log_1mb = jnp.where(strict, log_1mb, 0.0)
        hi = log_1mb.astype(BF16)
        lo = (log_1mb - hi.astype(F32)).astype(BF16)
        lb_ref[h] = log_beta
        first_ref[h] = log_1mb[0:1, :]
        loc_ref[h] = (jnp.dot(later, hi, preferred_element_type=F32)
                      + jnp.dot(later, lo, preferred_element_type=F32))

    def weigh(j, h, diagonal):
        local = loc_ref[h]
        c = c_ref[h]
        a = jnp.exp2(lb_ref[h] + (local + c))
        if diagonal:
            a = jnp.where(strict, a, 0.0)
        acc_ref[h] = acc_ref[h] + jnp.dot(vt_ref[h, j], a.astype(BF16), preferred_element_type=F32)
        c_ref[h] = c + (local[0:1, :] + first_ref[h])

    def step(j, diagonal, prefetch):
        for h in range(N_HEADS):
            decay(h, diagonal)
            if prefetch is not None:
                scores(prefetch, h)
        for h in range(N_HEADS):
            weigh(j, h, diagonal)

    for h in range(N_HEADS):
        scores(qi, h)

    @pl.when(qi == 0)
    def _():
        step(0, True, None)

    def any_weight_left():
        c_max = jnp.max(jnp.max(c_ref[...], axis=0), axis=1, keepdims=True)
        return c_max[0, 0] > DEAD_LOG2

    @pl.when(qi > 0)
    def _():
        step(qi, True, qi - 1)

        def body(state):
            j, _ = state
            step(j, False, jnp.maximum(j - 1, 0))
            return j - 1, any_weight_left()

        lax.while_loop(lambda state: jnp.logical_and(state[0] >= 0, state[1]), body,
                       (qi - 1, any_weight_left()))

    for h in range(N_HEADS):
        o_ref[h, 0] = acc_ref[h].T.astype(BF16)


def _attn_call(kernel, name, q, k, v, q_blk, k_blk, v_blk, extra=(), extra_specs=(), softmax=True):
    _, b, s, _ = q.shape
    t = ATT_BLOCK
    qw, kw, vw = q.shape[3], k.shape[3], v.shape[3]
    rows = HEAD_DIM + (SUM_ROWS if softmax else 0)
    vt = pltpu.VMEM((N_HEADS, s // t, rows, t), BF16)
    stat = pltpu.VMEM((N_HEADS, 1, t), F32)
    acc = pltpu.VMEM((N_HEADS, rows, t), F32)
    tile = pltpu.VMEM((N_HEADS, t, t), F32)
    return pl.pallas_call(
        kernel,
        grid=(b, s // t),
        in_specs=[pl.BlockSpec((N_HEADS, 1, t, qw), lambda i, j: (q_blk, i, j, 0)),
                  pl.BlockSpec((N_HEADS, 1, s, kw), lambda i, j: (k_blk, i, 0, 0)),
                  pl.BlockSpec((N_HEADS, 1, s, vw), lambda i, j: (v_blk, i, 0, 0)),
                  *extra_specs],
        out_specs=pl.BlockSpec((N_HEADS, 1, t, HEAD_DIM), lambda i, j: (0, i, j, 0)),
        out_shape=jax.ShapeDtypeStruct((N_HEADS, b, s, HEAD_DIM), BF16),
        scratch_shapes=[vt, stat, acc, tile] if softmax else [vt, stat, acc, tile, tile, tile, stat],
        compiler_params=_params("arbitrary", "arbitrary"),
        name=name,
    )(q, k, v, *extra)


def _merge_kernel(oa_ref, ob_ref, oc_ref, ga_ref, gb_ref, gc_ref, ma_ref, mb_ref, mc_ref, wb_ref,
                  out_ref, ys_ref):
    for n, (o_ref, g_ref) in enumerate(((oa_ref, ga_ref), (ob_ref, gb_ref), (oc_ref, gc_ref))):
        for h in range(N_HEADS):
            g = g_ref[h].astype(F32)
            ys_ref[n, :, h * HEAD_DIM:(h + 1) * HEAD_DIM] = (
                o_ref[h].astype(F32) * (g * jax.nn.sigmoid(g))).astype(BF16)

    gates = (ma_ref, mb_ref, mc_ref)
    for c0 in range(0, out_ref.shape[1], MERGE_CHUNK):
        branches = [jnp.dot(ys_ref[n], wb_ref[0, n, :, c0:c0 + MERGE_CHUNK], preferred_element_type=F32)
                    for n in range(N_BRANCH)]
        for off in range(0, MERGE_CHUNK, LANES):
            g = (c0 + off) // LANES
            acc = jax.nn.sigmoid(gates[0][g].astype(F32)) * branches[0][:, off:off + LANES]
            for n in range(1, N_BRANCH):
                acc = acc + jax.nn.sigmoid(gates[n][g].astype(F32)) * branches[n][:, off:off + LANES]
            out_ref[:, c0 + off:c0 + off + LANES] = acc.astype(BF16)


MERGE_CHUNK = 256


def _merge_call(o_a, o_b, o_c, proj, wb, blk, layer):
    n = proj.shape[1]
    d = wb.shape[3]
    tm = 512
    gpb = d // LANES
    heads = lambda c: pl.BlockSpec((N_HEADS, tm, LANES), lambda i: (c, i, 0))
    gate = lambda b: pl.BlockSpec((gpb, tm, LANES), lambda i: (b, i, 0))
    return pl.pallas_call(
        _merge_kernel,
        grid=(n // tm,),
        in_specs=[heads(0), heads(0), heads(0), heads(blk["g_a"]), heads(blk["g_b"]), heads(blk["g_c"]),
                  gate(0), gate(1), gate(2),
                  pl.BlockSpec((1, N_BRANCH, BRANCH_W, d), lambda i: (layer, 0, 0, 0),
                               pipeline_mode=pl.Buffered(1))],
        out_specs=pl.BlockSpec((tm, d), lambda i: (i, 0)),
        out_shape=jax.ShapeDtypeStruct((n, d), BF16),
        scratch_shapes=[pltpu.VMEM((N_BRANCH, tm, BRANCH_W), BF16)],
        compiler_params=_params("arbitrary"),
        name="gate_branch_merge",
    )(o_a, o_b, o_c, proj, proj, proj, proj, proj, proj, wb)


OUT_ROWS = 128


def _out_kernel(m_ref, w_ref, x_ref, gate_ref, g_ref, b_ref, o_ref, *, alpha):
    for r0 in range(0, m_ref.shape[0], OUT_ROWS):
        rows = slice(r0, r0 + OUT_ROWS)
        out = jnp.dot(m_ref[rows, :], w_ref[0], preferred_element_type=F32)
        r = alpha * x_ref[0, rows, :] + gate_ref[0] * out
        mu = jnp.mean(r, axis=-1, keepdims=True)
        rc = r - mu
        var = jnp.mean(rc * rc, axis=-1, keepdims=True)
        o_ref[0, rows, :] = rc * lax.rsqrt(var + LN_EPS) * g_ref[...] + b_ref[...]


def _out_call(merged, w_out, x, gate, ln_g, ln_b, alpha, layer):
    b, s, d = x.shape
    tm = 512
    per_seq = s // tm
    return pl.pallas_call(
        functools.partial(_out_kernel, alpha=alpha),
        grid=(b, per_seq),
        in_specs=[pl.BlockSpec((tm, d), lambda i, j: (i * per_seq + j, 0)),
                  pl.BlockSpec((1, d, d), lambda i, j: (layer, 0, 0)),
                  pl.BlockSpec((1, tm, d), lambda i, j: (i, j, 0)),
                  pl.BlockSpec((1, 1, d), lambda i, j: (i, 0, 0)),
                  pl.BlockSpec((1, d), lambda i, j: (0, 0)),
                  pl.BlockSpec((1, d), lambda i, j: (0, 0))],
        out_specs=pl.BlockSpec((1, tm, d), lambda i, j: (i, j, 0)),
        out_shape=jax.ShapeDtypeStruct((b, s, d), F32),
        compiler_params=_params("arbitrary", "arbitrary"),
        name="out_proj_ln",
    )(merged, w_out, x, gate, ln_g, ln_b)


def _swap_halves(w):
    half = w.shape[-1] // 2
    return jnp.concatenate([w[..., half:], w[..., :half]], axis=-1)


def _in_proj_columns(d):
    sizes = (Q_LORA, KV_LORA, MLA_ROPE, BRANCH_W, 3 * BRANCH_W, N_HEADS, BRANCH_W, 3 * BRANCH_W, BRANCH_W,
             N_BRANCH * d)
    starts = [sum(sizes[:i]) for i in range(len(sizes) + 1)]
    k_rope, g_a, f_logit, g_b, merge, end = starts[2], starts[3], starts[5], starts[6], starts[9], starts[10]
    runs = ((merge, end), (g_a, f_logit), (g_b, merge))
    return k_rope, f_logit, runs


def _latent_weights_kernel(w_ref, f_ref, lat_ref, *, k_rope):
    half = MLA_ROPE // 2
    swapped = k_rope + MLA_ROPE
    lat_ref[0, 0:swapped, :] = w_ref[0, 0:swapped, :].astype(BF16)
    lat_ref[0, swapped:swapped + half, :] = w_ref[0, k_rope + half:swapped, :].astype(BF16)
    lat_ref[0, swapped + half:swapped + MLA_ROPE, :] = w_ref[0, k_rope:k_rope + half, :].astype(BF16)
    pad = jnp.zeros((LANES - N_HEADS, w_ref.shape[2]), F32)
    lat_ref[0, swapped + MLA_ROPE:, :] = jnp.concatenate([f_ref[0], pad], axis=0).astype(BF16)


def _latent_weights_call(wt):
    depth, _, d = wt.shape
    k_rope, f_logit, _ = _in_proj_columns(d)
    assert k_rope + 2 * MLA_ROPE + LANES == LAT_W and f_logit % N_HEADS == 0
    tc = 4 * LANES
    return pl.pallas_call(
        functools.partial(_latent_weights_kernel, k_rope=k_rope),
        grid=(depth, d // tc),
        in_specs=[pl.BlockSpec((1, LAT_W, tc), lambda l, i: (l, 0, i)),
                  pl.BlockSpec((1, N_HEADS, tc), lambda l, i: (l, f_logit // N_HEADS, i))],
        out_specs=pl.BlockSpec((1, LAT_W, tc), lambda l, i: (l, 0, i)),
        out_shape=jax.ShapeDtypeStruct((depth, LAT_W, d), BF16),
        compiler_params=_params("arbitrary", "arbitrary"),
        name="latent_weights",
    )(wt, wt)


def _prepare_weights(w_in, w_uq, w_ukv, w_branch, w_out):
    depth = w_in.shape[0]
    w_main = jnp.swapaxes(w_in, 1, 2)
    w_lat = _latent_weights_call(w_main)
    uq = w_uq.reshape(depth, Q_LORA, N_HEADS, MLA_NOPE + MLA_ROPE)
    rope_cols = uq[..., MLA_NOPE:]
    uq = jnp.concatenate([uq[..., :MLA_NOPE], rope_cols, _swap_halves(rope_cols)], axis=-1)
    uq = uq.reshape(depth, Q_LORA, N_HEADS * MLA_QK).astype(BF16)
    return w_lat, w_main, uq, w_ukv.astype(BF16), w_branch.astype(BF16), w_out.astype(BF16)


def _proj_blocks(d):
    first = N_BRANCH * d // (N_HEADS * LANES)
    names = ("g_a", "q_b", "k_b", "v_b", "g_b", "q_c", "k_c", "v_c", "g_c")
    return {name: first + i for i, name in enumerate(names)}


def kernel(x, c, positions, w_ada, b_ada, w_in, q_norm_g, kv_norm_g, w_uq, w_ukv, fox_bias, w_branch, w_out,
           ln_g, ln_b):
    b, s, d = x.shape
    depth = w_in.shape[0]
    alpha = (2 * depth) ** 0.25
    blk = _proj_blocks(d)

    w_lat, w_main, uq, ukv, wb, wo = _prepare_weights(w_in, w_uq, w_ukv, w_branch, w_out)

    c_pad = jnp.zeros((8, d), BF16).at[:b].set(c.astype(BF16))
    mod = _ada_call(c_pad, w_ada, b_ada)[:, :b].reshape(depth, b, 3, 1, d)

    half = MLA_ROPE // 2
    inv_freq = ROPE_THETA ** (-jnp.arange(half, dtype=F32) / half)
    invf = jnp.concatenate([inv_freq, inv_freq, jnp.zeros((LANES - MLA_ROPE,), F32)]).reshape(1, LANES)
    cc, ss = _rope_call(positions, invf)

    fb = jnp.zeros((depth, 1, LANES), F32).at[:, 0, :N_HEADS].set(fox_bias)

    for l in range(depth):
        shift, scale, gate = mod[l, :, 0], mod[l, :, 1], mod[l, :, 2]
        u, q_a, k_a, v_a, frep, frow = _lat_call(
            x, shift, scale, w_lat, q_norm_g[l].reshape(1, -1), kv_norm_g[l].reshape(1, -1),
            uq, ukv, cc, ss, fb[l], l)
        proj = _proj_call(u.reshape(b * s, d), w_main, l)
        proj4 = proj.reshape(proj.shape[0], b, s, LANES)

        o_a = _attn_call(functools.partial(_softmax_kernel, forget=False), "mla_attention",
                         q_a, k_a, v_a, 0, 0, 0)
        t = ATT_BLOCK
        o_b = _attn_call(functools.partial(_softmax_kernel, forget=True), "fox_attention",
                         proj4, proj4, proj4, blk["q_b"], blk["k_b"], blk["v_b"],
                         extra=(frep, frow),
                         extra_specs=(pl.BlockSpec((N_HEADS, 1, s, LANES), lambda i, j: (0, i, 0, 0)),
                                      pl.BlockSpec((1, s // t, N_HEADS, t), lambda i, j: (i, 0, 0, 0))))
        o_c = _attn_call(_stick_kernel, "stick_attention",
                         proj4, proj4, proj4, blk["q_c"], blk["k_c"], blk["v_c"], softmax=False)

        merged = _merge_call(o_a.reshape(N_HEADS, b * s, LANES), o_b.reshape(N_HEADS, b * s, LANES),
                             o_c.reshape(N_HEADS, b * s, LANES), proj, wb, blk, l)
        x = _out_call(merged, wo, x, gate, ln_g[l].reshape(1, d), ln_b[l].reshape(1, d), alpha, l)
    return x
```

```python
import functools
import math

import jax
import jax.numpy as jnp
from jax import lax
from jax.experimental import pallas as pl
from jax.experimental.pallas import tpu as pltpu

F32 = jnp.float32
BF16 = jnp.bfloat16

N_HEADS = 8
HEAD_DIM = 128
MLA_NOPE = 128
MLA_ROPE = 64
MLA_QK = 2 * HEAD_DIM
Q_LORA = 512
KV_LORA = 256
CHUNK = 64
N_BRANCH = 3
BRANCH_W = N_HEADS * HEAD_DIM
ROPE_THETA = 10000.0
LN_EPS = 1e-5
RMS_EPS = 1e-6
NEG_INF = -1e30
LOG2E = math.log2(math.e)
LANES = 128
SUBLANES = 8
LAT_W = 1024
VMEM_LIMIT = 52 * 1024 * 1024

ATT_BLOCK = 256
LAT_ROWS = 256


def _params(*sem):
    return pltpu.CompilerParams(dimension_semantics=sem, vmem_limit_bytes=VMEM_LIMIT)


def _log_sigmoid(x):
    return jnp.minimum(x, 0.0) - jnp.log1p(jnp.exp(-jnp.abs(x)))


def _ada_kernel(c_ref, w_ref, b_ref, o_ref):
    w = w_ref[0].astype(BF16)
    o_ref[0] = jnp.dot(c_ref[...], w, preferred_element_type=F32) + b_ref[0]


def _ada_call(c_pad, w_ada, b_ada):
    depth, d, n3 = w_ada.shape
    tn = 1024
    return pl.pallas_call(
        _ada_kernel,
        grid=(depth, n3 // tn),
        in_specs=[pl.BlockSpec((8, d), lambda l, j: (0, 0)),
                  pl.BlockSpec((1, d, tn), lambda l, j: (l, 0, j)),
                  pl.BlockSpec((1, 1, tn), lambda l, j: (l, 0, j))],
        out_specs=pl.BlockSpec((1, 8, tn), lambda l, j: (l, 0, j)),
        out_shape=jax.ShapeDtypeStruct((depth, 8, n3), F32),
        compiler_params=_params("arbitrary", "arbitrary"),
        name="ada_mod",
    )(c_pad, w_ada, b_ada.reshape(depth, 1, n3))


def _rope_kernel(pos_ref, invf_ref, cc_ref, ss_ref):
    ang = pos_ref[0].astype(F32) * invf_ref[...]
    lane = lax.broadcasted_iota(jnp.int32, ang.shape, 1)
    cos = jnp.cos(ang)
    sin = jnp.sin(ang)
    half = MLA_ROPE // 2
    cc_ref[0] = jnp.where(lane < MLA_ROPE, cos, 0.0)
    ss_ref[0] = jnp.where(lane < half, -sin, jnp.where(lane < MLA_ROPE, sin, 0.0))


def _rope_call(positions, invf):
    b, s = positions.shape
    ts = 512
    spec = pl.BlockSpec((1, ts, LANES), lambda i, j: (i, j, 0))
    return pl.pallas_call(
        _rope_kernel,
        grid=(b, s // ts),
        in_specs=[pl.BlockSpec((1, ts, 1), lambda i, j: (i, j, 0)),
                  pl.BlockSpec((1, LANES), lambda i, j: (0, 0))],
        out_specs=[spec, spec],
        out_shape=[jax.ShapeDtypeStruct((b, s, LANES), F32)] * 2,
        compiler_params=_params("arbitrary", "arbitrary"),
        name="rope_tables",
    )(positions.reshape(b, s, 1), invf)


def _rope_apply(r, cc, ss):
    return r * cc + pltpu.roll(r, MLA_ROPE, 1) * ss


def _lat_kernel(x_ref, shift_ref, scale_ref, wlat_ref, qg_ref, kvg_ref, wuq_ref, wukv_ref,
                cc_ref, ss_ref, fb_ref,
                u_ref, q_ref, k_ref, v_ref, fkey_ref, fqry_ref, carry_ref):
    ts = x_ref.shape[1]
    x = x_ref[0]
    mu = jnp.mean(x, axis=-1, keepdims=True)
    xc = x - mu
    var = jnp.mean(xc * xc, axis=-1, keepdims=True)
    y = xc * lax.rsqrt(var + LN_EPS)
    u = (y * (1.0 + scale_ref[0]) + shift_ref[0]).astype(BF16)
    u_ref[0] = u

    lat = _nt(u, wlat_ref[0])
    cc = cc_ref[0]
    ss = ss_ref[0]
    q_scale = LOG2E / math.sqrt(MLA_NOPE + MLA_ROPE)

    cq = lat[:, :Q_LORA]
    cqn = cq * lax.rsqrt(jnp.mean(cq * cq, axis=-1, keepdims=True) + RMS_EPS) * qg_ref[...]
    q = jnp.dot(cqn.astype(BF16), wuq_ref[0], preferred_element_type=F32)
    for h in range(N_HEADS):
        base = h * MLA_QK
        q_ref[h, 0, :, 0:HEAD_DIM] = (q[:, base:base + HEAD_DIM] * q_scale).astype(BF16)
        r = q[:, base + HEAD_DIM:base + MLA_QK]
        q_ref[h, 0, :, HEAD_DIM:MLA_QK] = (_rope_apply(r, cc, ss) * q_scale).astype(BF16)

    ckv = lat[:, Q_LORA:Q_LORA + KV_LORA]
    ckvn = ckv * lax.rsqrt(jnp.mean(ckv * ckv, axis=-1, keepdims=True) + RMS_EPS) * kvg_ref[...]
    kv = jnp.dot(ckvn.astype(BF16), wukv_ref[0], preferred_element_type=F32)
    r_off = Q_LORA + KV_LORA
    kpe = _rope_apply(lat[:, r_off:r_off + LANES], cc, ss).astype(BF16)
    for h in range(N_HEADS):
        base = h * 2 * HEAD_DIM
        k_ref[h, 0, :, 0:HEAD_DIM] = kv[:, base:base + HEAD_DIM].astype(BF16)
        k_ref[h, 0, :, HEAD_DIM:MLA_QK] = kpe
        v_ref[h, 0] = kv[:, base + HEAD_DIM:base + 2 * HEAD_DIM].astype(BF16)

    f_off = r_off + LANES
    logf = _log_sigmoid(lat[:, f_off:f_off + LANES] + fb_ref[...])
    lane = lax.broadcasted_iota(jnp.int32, logf.shape, 1)
    logf = jnp.where(lane < N_HEADS, logf, 0.0)
    row = lax.broadcasted_iota(jnp.int32, (ts, ts), 0)
    col = lax.broadcasted_iota(jnp.int32, (ts, ts), 1)
    tri = jnp.where(row >= col, 1.0, 0.0).astype(BF16)
    p1 = logf.astype(BF16)
    r1 = logf - p1.astype(F32)
    p2 = r1.astype(BF16)
    p3 = (r1 - p2.astype(F32)).astype(BF16)
    cs = (jnp.dot(tri, p1, preferred_element_type=F32)
          + jnp.dot(tri, p2, preferred_element_type=F32)
          + jnp.dot(tri, p3, preferred_element_type=F32))

    @pl.when(pl.program_id(1) == 0)
    def _():
        carry_ref[...] = jnp.zeros_like(carry_ref)

    fc = cs + carry_ref[...]
    carry_ref[...] = fc[ts - 1:ts, :]
    fc2 = fc * LOG2E
    hi = fc2.astype(BF16).astype(F32)
    r1 = fc2 - hi
    mid = r1.astype(BF16).astype(F32)
    low = (r1 - mid).astype(BF16).astype(F32)
    pieces = hi + pltpu.roll(mid, N_HEADS, 1) + pltpu.roll(low, 2 * N_HEADS, 1)
    fkey_ref[0] = pieces.astype(BF16)
    fqry_ref[0] = pltpu.roll(pieces, 3 * N_HEADS, 1).astype(BF16)


def _lat_call(x, shift, scale, wlat, qg, kvg, wuq, wukv, cc, ss, fb, layer):
    b, s, d = x.shape
    ts = LAT_ROWS
    const = lambda shape: pl.BlockSpec(shape, lambda i, j: (0,) * len(shape))
    stacked = lambda shape: pl.BlockSpec((1,) + shape, lambda i, j: (layer,) + (0,) * len(shape))
    tok = lambda w: pl.BlockSpec((1, ts, w), lambda i, j: (i, j, 0))
    per_batch = pl.BlockSpec((1, 1, d), lambda i, j: (i, 0, 0))
    head = lambda w: pl.BlockSpec((N_HEADS, 1, ts, w), lambda i, j: (0, i, j, 0))
    return pl.pallas_call(
        _lat_kernel,
        grid=(b, s // ts),
        in_specs=[tok(d), per_batch, per_batch, stacked((LAT_W, d)), const((1, Q_LORA)), const((1, KV_LORA)),
                  stacked((Q_LORA, N_HEADS * MLA_QK)), stacked((KV_LORA, N_HEADS * 2 * HEAD_DIM)),
                  tok(LANES), tok(LANES), const((1, LANES))],
        out_specs=[tok(d), head(MLA_QK), head(MLA_QK), head(HEAD_DIM), tok(LANES), tok(LANES)],
        out_shape=[jax.ShapeDtypeStruct((b, s, d), BF16),
                   jax.ShapeDtypeStruct((N_HEADS, b, s, MLA_QK), BF16),
                   jax.ShapeDtypeStruct((N_HEADS, b, s, MLA_QK), BF16),
                   jax.ShapeDtypeStruct((N_HEADS, b, s, HEAD_DIM), BF16),
                   jax.ShapeDtypeStruct((b, s, LANES), BF16),
                   jax.ShapeDtypeStruct((b, s, LANES), BF16)],
        scratch_shapes=[pltpu.VMEM((1, LANES), F32)],
        compiler_params=_params("arbitrary", "arbitrary"),
        name="ln_latent",
    )(x, shift, scale, wlat, qg, kvg, wuq, wukv, cc, ss, fb)


def _nt(a, b):
    return lax.dot_general(a, b, (((1,), (1,)), ((), ())), preferred_element_type=F32)


def _proj_kernel(u_ref, w_ref, o_ref, wb_ref, *, scaled_tiles):
    j = pl.program_id(0)

    @pl.when(pl.program_id(1) == 0)
    def _():
        is_scaled = functools.reduce(jnp.logical_or, [j == t for t in scaled_tiles])
        scale = jnp.where(is_scaled, LOG2E / math.sqrt(HEAD_DIM), 1.0)
        wb_ref[...] = (w_ref[0] * scale).astype(BF16)

    acc = _nt(u_ref[...], wb_ref[...])
    for g in range(o_ref.shape[0]):
        o_ref[g] = acc[:, g * LANES:(g + 1) * LANES].astype(BF16)


def _proj_call(u, wt, layer):
    n, d = u.shape
    _, _, runs = _in_proj_columns(d)
    tm, tn = 1024, 1024
    ncol = sum(hi - lo for lo, hi in runs)
    assert all((hi - lo) % tn == 0 for lo, hi in runs)
    q_b = (N_BRANCH * d + BRANCH_W) // tn
    q_c = q_b + 4 * BRANCH_W // tn

    tile0, firsts = 0, []
    for lo, hi in runs:
        firsts.append((tile0, lo))
        tile0 += (hi - lo) // tn

    def first_row(j):
        t_last, lo_last = firsts[-1]
        row = lo_last + (j - t_last) * tn
        for k in range(len(firsts) - 2, -1, -1):
            t_k, lo_k = firsts[k]
            row = jnp.where(j < firsts[k + 1][0], lo_k + (j - t_k) * tn, row)
        return pl.multiple_of(row, SUBLANES)

    return pl.pallas_call(
        functools.partial(_proj_kernel, scaled_tiles=(q_b, q_c)),
        grid=(ncol // tn, n // tm),
        in_specs=[pl.BlockSpec((tm, d), lambda j, i: (i, 0)),
                  pl.BlockSpec((pl.Element(1), pl.Element(tn), pl.Element(d)),
                               lambda j, i: (layer, first_row(j), 0))],
        out_specs=pl.BlockSpec((tn // LANES, tm, LANES), lambda j, i: (j, i, 0)),
        out_shape=jax.ShapeDtypeStruct((ncol // LANES, n, LANES), BF16),
        scratch_shapes=[pltpu.VMEM((tn, d), BF16)],
        compiler_params=_params("arbitrary", "arbitrary"),
        name="in_proj",
    )(u, wt)


def _block_iotas(t):
    row = lax.broadcasted_iota(jnp.int32, (t, t), 0)
    col = lax.broadcasted_iota(jnp.int32, (t, t), 1)
    return row, col


def _kq(k, q):
    return lax.dot_general(k, q, (((1,), (1,)), ((), ())), preferred_element_type=F32)


def _stage_values(v_ref, vt_ref, ones_rows):
    n_blk, t = vt_ref.shape[1], vt_ref.shape[3]

    @pl.when(pl.program_id(1) == 0)
    def _():
        def per_head(h, carry):
            for j in range(n_blk):
                vt = v_ref[h, 0, j * t:(j + 1) * t, :].astype(F32).T
                vt_ref[h, j, 0:HEAD_DIM, :] = vt.astype(BF16)
                if ones_rows:
                    vt_ref[h, j, HEAD_DIM:HEAD_DIM + ones_rows, :] = jnp.ones((ones_rows, t), BF16)
            return carry
        lax.fori_loop(0, N_HEADS, per_head, 0)


SUM_ROWS = 8
DEAD_LOG2 = -150.0


def _softmax_kernel(*refs, forget):
    if forget:
        q_ref, k_ref, v_ref, fkey_ref, fqry_ref, o_ref, vt_ref, m_ref, acc_ref, s_ref = refs
    else:
        q_ref, k_ref, v_ref, o_ref, vt_ref, m_ref, acc_ref, s_ref = refs
    t = q_ref.shape[2]
    qi = pl.program_id(1)
    key, qry = _block_iotas(t)
    mask = (key <= qry) if forget else ((key // CHUNK) <= (qry // CHUNK))

    _stage_values(v_ref, vt_ref, SUM_ROWS)
    m_ref[...] = jnp.full(m_ref.shape, NEG_INF, F32)
    acc_ref[...] = jnp.zeros(acc_ref.shape, F32)

    if forget:
        lane = lax.broadcasted_iota(jnp.int32, (1, LANES), 1)

        def picks(h, first, value):
            hit = functools.reduce(jnp.logical_or, [lane == first + p * N_HEADS + h for p in range(3)])
            return jnp.where(hit, value, 0.0)

    def scores(j, h):
        off = pl.multiple_of(j * t, t)
        k = k_ref[h, 0, pl.ds(off, t), :]
        q = q_ref[h, 0]
        if forget:
            k2 = fkey_ref[0, pl.ds(off, t), :].astype(F32) + picks(h, 3 * N_HEADS, 1.0)
            q2 = fqry_ref[0].astype(F32) + picks(h, 0, -1.0)
            k = jnp.concatenate([k, k2.astype(BF16)], axis=1)
            q = jnp.concatenate([q, q2.astype(BF16)], axis=1)
        s_ref[h] = _kq(k, q)

    def consume(j, h, diagonal):
        s = s_ref[h]
        if diagonal:
            s = jnp.where(mask, s, NEG_INF)
        m_old = m_ref[h]
        m_new = jnp.maximum(m_old, jnp.max(s, axis=0, keepdims=True))
        p = jnp.exp2(s - m_new).astype(BF16)
        acc_ref[h] = (jnp.exp2(m_old - m_new) * acc_ref[h]
                      + jnp.dot(vt_ref[h, j], p, preferred_element_type=F32))
        m_ref[h] = m_new

    for h in range(N_HEADS):
        scores(0, h)

    def body(j, carry):
        for h in range(N_HEADS):
            consume(j, h, False)
            scores(j + 1, h)
        return carry

    lax.fori_loop(0, qi, body, 0)
    for h in range(N_HEADS):
        consume(qi, h, True)
        acc = acc_ref[h]
        o_ref[h, 0] = (acc[:HEAD_DIM] / acc[HEAD_DIM:HEAD_DIM + 1]).T.astype(BF16)


def _stick_kernel(q_ref, k_ref, v_ref, o_ref, vt_ref, c_ref, acc_ref, s_ref, lb_ref, loc_ref, first_ref):
    t = q_ref.shape[2]
    qi = pl.program_id(1)
    key, qry = _block_iotas(t)
    strict = key < qry
    later = jnp.where(qry > key, 1.0, 0.0).astype(BF16)

    _stage_values(v_ref, vt_ref, 0)
    c_ref[...] = jnp.zeros(c_ref.shape, F32)
    acc_ref[...] = jnp.zeros(acc_ref.shape, F32)

    def scores(j, h):
        off = pl.multiple_of(j * t, t)
        s_ref[h] = _kq(k_ref[h, 0, pl.ds(off, t), :], q_ref[h, 0])

    def decay(h, diagonal):
        z = s_ref[h]
        log_beta = jnp.minimum(z, 0.0) - jnp.log2(1.0 + jnp.exp2(-jnp.abs(z)))
        log_1mb = log_beta - z
        if diagonal:
            log_1mb = jnp.where(strict, log_1mb, 0.0)
        hi = log_1mb.astype(BF16)
        lo = (log_1mb - hi.astype(F32)).astype(BF16)
        lb_ref[h] = log_beta
        first_ref[h] = log_1mb[0:1, :]
        loc_ref[h] = (jnp.dot(later, hi, preferred_element_type=F32)
                      + jnp.dot(later, lo, preferred_element_type=F32))

    def weigh(j, h, diagonal):
        local = loc_ref[h]
        c = c_ref[h]
        a = jnp.exp2(lb_ref[h] + (local + c))
        if diagonal:
            a = jnp.where(strict, a, 0.0)
        acc_ref[h] = acc_ref[h] + jnp.dot(vt_ref[h, j], a.astype(BF16), preferred_element_type=F32)
        c_ref[h] = c + (local[0:1, :] + first_ref[h])

    def step(j, diagonal, prefetch):
        for h in range(N_HEADS):
            decay(h, diagonal)
            if prefetch is not None:
                scores(prefetch, h)
        for h in range(N_HEADS):
            weigh(j, h, diagonal)

    for h in range(N_HEADS):
        scores(qi, h)

    @pl.when(qi == 0)
    def _():
        step(0, True, None)

    def any_weight_left():
        c_max = jnp.max(jnp.max(c_ref[...], axis=0), axis=1, keepdims=True)
        return c_max[0, 0] > DEAD_LOG2

    @pl.when(qi > 0)
    def _():
        step(qi, True, qi - 1)

        def body(state):
            j, _ = state
            step(j, False, jnp.maximum(j - 1, 0))
            return j - 1, any_weight_left()

        lax.while_loop(lambda state: jnp.logical_and(state[0] >= 0, state[1]), body,
                       (qi - 1, any_weight_left()))

    for h in range(N_HEADS):
        o_ref[h, 0] = acc_ref[h].T.astype(BF16)


def _attn_call(kernel, name, q, k, v, q_blk, k_blk, v_blk, extra=(), extra_specs=(), softmax=True):
    _, b, s, _ = q.shape
    t = ATT_BLOCK
    qw, kw, vw = q.shape[3], k.shape[3], v.shape[3]
    rows = HEAD_DIM + (SUM_ROWS if softmax else 0)
    vt = pltpu.VMEM((N_HEADS, s // t, rows, t), BF16)
    stat = pltpu.VMEM((N_HEADS, 1, t), F32)
    acc = pltpu.VMEM((N_HEADS, rows, t), F32)
    tile = pltpu.VMEM((N_HEADS, t, t), F32)
    return pl.pallas_call(
        kernel,
        grid=(b, s // t),
        in_specs=[pl.BlockSpec((N_HEADS, 1, t, qw), lambda i, j: (q_blk, i, j, 0)),
                  pl.BlockSpec((N_HEADS, 1, s, kw), lambda i, j: (k_blk, i, 0, 0)),
                  pl.BlockSpec((N_HEADS, 1, s, vw), lambda i, j: (v_blk, i, 0, 0)),
                  *extra_specs],
        out_specs=pl.BlockSpec((N_HEADS, 1, t, HEAD_DIM), lambda i, j: (0, i, j, 0)),
        out_shape=jax.ShapeDtypeStruct((N_HEADS, b, s, HEAD_DIM), BF16),
        scratch_shapes=[vt, stat, acc, tile] if softmax else [vt, stat, acc, tile, tile, tile, stat],
        compiler_params=_params("arbitrary", "arbitrary"),
        name=name,
    )(q, k, v, *extra)


def _merge_kernel(oa_ref, ob_ref, oc_ref, ga_ref, gb_ref, gc_ref, ma_ref, mb_ref, mc_ref, wb_ref,
                  out_ref, ys_ref):
    for n, (o_ref, g_ref) in enumerate(((oa_ref, ga_ref), (ob_ref, gb_ref), (oc_ref, gc_ref))):
        for h in range(N_HEADS):
            g = g_ref[h].astype(F32)
            ys_ref[n, :, h * HEAD_DIM:(h + 1) * HEAD_DIM] = (
                o_ref[h].astype(F32) * (g * jax.nn.sigmoid(g))).astype(BF16)

    gates = (ma_ref, mb_ref, mc_ref)
    for c0 in range(0, out_ref.shape[1], MERGE_CHUNK):
        branches = [jnp.dot(ys_ref[n], wb_ref[0, n, :, c0:c0 + MERGE_CHUNK], preferred_element_type=F32)
                    for n in range(N_BRANCH)]
        for off in range(0, MERGE_CHUNK, LANES):
            g = (c0 + off) // LANES
            acc = jax.nn.sigmoid(gates[0][g].astype(F32)) * branches[0][:, off:off + LANES]
            for n in range(1, N_BRANCH):
                acc = acc + jax.nn.sigmoid(gates[n][g].astype(F32)) * branches[n][:, off:off + LANES]
            out_ref[:, c0 + off:c0 + off + LANES] = acc.astype(BF16)


MERGE_CHUNK = 256


def _merge_call(o_a, o_b, o_c, proj, wb, blk, layer):
    n = proj.shape[1]
    d = wb.shape[3]
    tm = 512
    gpb = d // LANES
    heads = lambda c: pl.BlockSpec((N_HEADS, tm, LANES), lambda i: (c, i, 0))
    gate = lambda b: pl.BlockSpec((gpb, tm, LANES), lambda i: (b, i, 0))
    return pl.pallas_call(
        _merge_kernel,
        grid=(n // tm,),
        in_specs=[heads(0), heads(0), heads(0), heads(blk["g_a"]), heads(blk["g_b"]), heads(blk["g_c"]),
                  gate(0), gate(1), gate(2),
                  pl.BlockSpec((1, N_BRANCH, BRANCH_W, d), lambda i: (layer, 0, 0, 0),
                               pipeline_mode=pl.Buffered(1))],
        out_specs=pl.BlockSpec((tm, d), lambda i: (i, 0)),
        out_shape=jax.ShapeDtypeStruct((n, d), BF16),
        scratch_shapes=[pltpu.VMEM((N_BRANCH, tm, BRANCH_W), BF16)],
        compiler_params=_params("arbitrary"),
        name="gate_branch_merge",
    )(o_a, o_b, o_c, proj, proj, proj, proj, proj, proj, wb)


OUT_ROWS = 128


def _out_kernel(m_ref, w_ref, x_ref, gate_ref, g_ref, b_ref, o_ref, *, alpha):
    for r0 in range(0, m_ref.shape[0], OUT_ROWS):
        rows = slice(r0, r0 + OUT_ROWS)
        out = jnp.dot(m_ref[rows, :], w_ref[0], preferred_element_type=F32)
        r = alpha * x_ref[0, rows, :] + gate_ref[0] * out
        mu = jnp.mean(r, axis=-1, keepdims=True)
        rc = r - mu
        var = jnp.mean(rc * rc, axis=-1, keepdims=True)
        o_ref[0, rows, :] = rc * lax.rsqrt(var + LN_EPS) * g_ref[...] + b_ref[...]


def _out_call(merged, w_out, x, gate, ln_g, ln_b, alpha, layer):
    b, s, d = x.shape
    tm = 512
    per_seq = s // tm
    return pl.pallas_call(
        functools.partial(_out_kernel, alpha=alpha),
        grid=(b, per_seq),
        in_specs=[pl.BlockSpec((tm, d), lambda i, j: (i * per_seq + j, 0)),
                  pl.BlockSpec((1, d, d), lambda i, j: (layer, 0, 0)),
                  pl.BlockSpec((1, tm, d), lambda i, j: (i, j, 0)),
                  pl.BlockSpec((1, 1, d), lambda i, j: (i, 0, 0)),
                  pl.BlockSpec((1, d), lambda i, j: (0, 0)),
                  pl.BlockSpec((1, d), lambda i, j: (0, 0))],
        out_specs=pl.BlockSpec((1, tm, d), lambda i, j: (i, j, 0)),
        out_shape=jax.ShapeDtypeStruct((b, s, d), F32),
        compiler_params=_params("arbitrary", "arbitrary"),
        name="out_proj_ln",
    )(merged, w_out, x, gate, ln_g, ln_b)


def _swap_halves(w):
    half = w.shape[-1] // 2
    return jnp.concatenate([w[..., half:], w[..., :half]], axis=-1)


def _in_proj_columns(d):
    sizes = (Q_LORA, KV_LORA, MLA_ROPE, BRANCH_W, 3 * BRANCH_W, N_HEADS, BRANCH_W, 3 * BRANCH_W, BRANCH_W,
             N_BRANCH * d)
    starts = [sum(sizes[:i]) for i in range(len(sizes) + 1)]
    k_rope, g_a, f_logit, g_b, merge, end = starts[2], starts[3], starts[5], starts[6], starts[9], starts[10]
    runs = ((merge, end), (g_a, f_logit), (g_b, merge))
    return k_rope, f_logit, runs


def _latent_weights_kernel(w_ref, f_ref, lat_ref, *, k_rope):
    half = MLA_ROPE // 2
    swapped = k_rope + MLA_ROPE
    lat_ref[0, 0:swapped, :] = w_ref[0, 0:swapped, :].astype(BF16)
    lat_ref[0, swapped:swapped + half, :] = w_ref[0, k_rope + half:swapped, :].astype(BF16)
    lat_ref[0, swapped + half:swapped + MLA_ROPE, :] = w_ref[0, k_rope:k_rope + half, :].astype(BF16)
    pad = jnp.zeros((LANES - N_HEADS, w_ref.shape[2]), F32)
    lat_ref[0, swapped + MLA_ROPE:, :] = jnp.concatenate([f_ref[0], pad], axis=0).astype(BF16)


def _latent_weights_call(wt):
    depth, _, d = wt.shape
    k_rope, f_logit, _ = _in_proj_columns(d)
    assert k_rope + 2 * MLA_ROPE + LANES == LAT_W and f_logit % N_HEADS == 0
    tc = 4 * LANES
    return pl.pallas_call(
        functools.partial(_latent_weights_kernel, k_rope=k_rope),
        grid=(depth, d // tc),
        in_specs=[pl.BlockSpec((1, LAT_W, tc), lambda l, i: (l, 0, i)),
                  pl.BlockSpec((1, N_HEADS, tc), lambda l, i: (l, f_logit // N_HEADS, i))],
        out_specs=pl.BlockSpec((1, LAT_W, tc), lambda l, i: (l, 0, i)),
        out_shape=jax.ShapeDtypeStruct((depth, LAT_W, d), BF16),
        compiler_params=_params("arbitrary", "arbitrary"),
        name="latent_weights",
    )(wt, wt)


def _prepare_weights(w_in, w_uq, w_ukv, w_branch, w_out):
    depth = w_in.shape[0]
    w_main = jnp.swapaxes(w_in, 1, 2)
    w_lat = _latent_weights_call(w_main)
    uq = w_uq.reshape(depth, Q_LORA, N_HEADS, MLA_NOPE + MLA_ROPE)
    rope_cols = uq[..., MLA_NOPE:]
    uq = jnp.concatenate([uq[..., :MLA_NOPE], rope_cols, _swap_halves(rope_cols)], axis=-1)
    uq = uq.reshape(depth, Q_LORA, N_HEADS * MLA_QK).astype(BF16)
    return w_lat, w_main, uq, w_ukv.astype(BF16), w_branch.astype(BF16), w_out.astype(BF16)


def _proj_blocks(d):
    first = N_BRANCH * d // (N_HEADS * LANES)
    names = ("g_a", "q_b", "k_b", "v_b", "g_b", "q_c", "k_c", "v_c", "g_c")
    return {name: first + i for i, name in enumerate(names)}


def kernel(x, c, positions, w_ada, b_ada, w_in, q_norm_g, kv_norm_g, w_uq, w_ukv, fox_bias, w_branch, w_out,
           ln_g, ln_b):
    b, s, d = x.shape
    depth = w_in.shape[0]
    alpha = (2 * depth) ** 0.25
    blk = _proj_blocks(d)

    w_lat, w_main, uq, ukv, wb, wo = _prepare_weights(w_in, w_uq, w_ukv, w_branch, w_out)

    c_pad = jnp.zeros((8, d), BF16).at[:b].set(c.astype(BF16))
    mod = _ada_call(c_pad, w_ada, b_ada)[:, :b].reshape(depth, b, 3, 1, d)

    half = MLA_ROPE // 2
    inv_freq = ROPE_THETA ** (-jnp.arange(half, dtype=F32) / half)
    invf = jnp.concatenate([inv_freq, inv_freq, jnp.zeros((LANES - MLA_ROPE,), F32)]).reshape(1, LANES)
    cc, ss = _rope_call(positions, invf)

    fb = jnp.zeros((depth, 1, LANES), F32).at[:, 0, :N_HEADS].set(fox_bias)

    for l in range(depth):
        shift, scale, gate = mod[l, :, 0], mod[l, :, 1], mod[l, :, 2]
        u, q_a, k_a, v_a, fkey, fqry = _lat_call(
            x, shift, scale, w_lat, q_norm_g[l].reshape(1, -1), kv_norm_g[l].reshape(1, -1),
            uq, ukv, cc, ss, fb[l], l)
        proj = _proj_call(u.reshape(b * s, d), w_main, l)
        proj4 = proj.reshape(proj.shape[0], b, s, LANES)

        o_a = _attn_call(functools.partial(_softmax_kernel, forget=False), "mla_attention",
                         q_a, k_a, v_a, 0, 0, 0)
        t = ATT_BLOCK
        o_b = _attn_call(functools.partial(_softmax_kernel, forget=True), "fox_attention",
                         proj4, proj4, proj4, blk["q_b"], blk["k_b"], blk["v_b"],
                         extra=(fkey, fqry),
                         extra_specs=(pl.BlockSpec((1, s, LANES), lambda i, j: (i, 0, 0)),
                                      pl.BlockSpec((1, t, LANES), lambda i, j: (i, j, 0))))
        o_c = _attn_call(_stick_kernel, "stick_attention",
                         proj4, proj4, proj4, blk["q_c"], blk["k_c"], blk["v_c"], softmax=False)

        merged = _merge_call(o_a.reshape(N_HEADS, b * s, LANES), o_b.reshape(N_HEADS, b * s, LANES),
                             o_c.reshape(N_HEADS, b * s, LANES), proj, wb, blk, l)
        x = _out_call(merged, wo, x, gate, ln_g[l].reshape(1, d), ln_b[l].reshape(1, d), alpha, l)
    return x
```
